```python
import jax, jax.numpy as jnp
from jax import lax
import numpy as np

D_MODEL = 2048
BATCH = 2
SEQ = 8192
DEPTH = 1

N_MEM = 256
MLA_HEADS = 16
QK_NOPE_DIM = 128
QK_ROPE_DIM = 64
V_HEAD_DIM = 128
KV_LORA_RANK = 512
QK_HEAD_DIM = QK_NOPE_DIM + QK_ROPE_DIM
SOFTMAX_SCALE = QK_HEAD_DIM ** -0.5
ROPE_THETA = 10000.0
Q_BLOCK = 128
LRU_WIDTH = D_MODEL
LRU_BLOCKS = 16
LRU_BLOCK_DIM = LRU_WIDTH // LRU_BLOCKS
LRU_CONV_WIDTH = 4
LRU_C = 8.0
X_HEADS = 4
X_HEAD_DIM = 128
D_FF = 5632
FFN_CONV_WIDTH = 3
EPS = 1e-6

Q_COLS = MLA_HEADS * QK_HEAD_DIM
IN_SIZES = (Q_COLS, KV_LORA_RANK, QK_ROPE_DIM, LRU_WIDTH, LRU_WIDTH, D_MODEL, D_MODEL)
IN_COLS = sum(IN_SIZES)
IN_SPLIT_IDX = tuple(int(v) for v in np.cumsum(IN_SIZES)[:-1])

kernel_name = "hybrid_mla_rglru_gated_merge_layer"


def rmsnorm(x, g):
    x32 = x.astype(jnp.float32)
    y = x32 * lax.rsqrt(jnp.mean(x32 * x32, axis=-1, keepdims=True) + EPS)
    return (y * g.astype(jnp.float32)).astype(x.dtype)


def rope_tables(positions):
    inv_freq = ROPE_THETA ** (-jnp.arange(0, QK_ROPE_DIM, 2, dtype=jnp.float32) / QK_ROPE_DIM)
    ang = positions.astype(jnp.float32)[..., None] * inv_freq
    return jnp.cos(ang), jnp.sin(ang)


def apply_rope(x, cos, sin):
    x32 = x.astype(jnp.float32)
    x1, x2 = jnp.split(x32, 2, axis=-1)
    out = jnp.concatenate([x1 * cos - x2 * sin, x2 * cos + x1 * sin], axis=-1)
    return out.astype(x.dtype)


def causal_dwconv(x, w, b):
    k_width = w.shape[0]
    s = x.shape[1]
    xp = jnp.pad(x, ((0, 0), (k_width - 1, 0), (0, 0)))
    acc = b
    for k in range(k_width):
        acc = acc + xp[:, k:k + s] * w[k]
    return acc


def mla_attention(q_in, ckv, k_rope, positions, g_ckv, w_ukv):
    b, s, _ = q_in.shape
    cos, sin = rope_tables(positions)
    q = q_in.reshape(b, s, MLA_HEADS, QK_HEAD_DIM)
    q_nope, q_rope = q[..., :QK_NOPE_DIM], q[..., QK_NOPE_DIM:]
    q_rope = apply_rope(q_rope, cos[:, :, None], sin[:, :, None])
    q = jnp.concatenate([q_nope, q_rope], axis=-1)
    k_rope = apply_rope(k_rope, cos, sin)
    kv = (rmsnorm(ckv, g_ckv) @ w_ukv).reshape(b, s, MLA_HEADS, QK_NOPE_DIM + V_HEAD_DIM)
    k_nope, v = kv[..., :QK_NOPE_DIM], kv[..., QK_NOPE_DIM:]
    k = jnp.concatenate([k_nope, jnp.broadcast_to(k_rope[:, :, None], (b, s, MLA_HEADS, QK_ROPE_DIM))], axis=-1)
    n_blocks = s // Q_BLOCK
    q_blocks = q.reshape(b, n_blocks, Q_BLOCK, MLA_HEADS, QK_HEAD_DIM).transpose(1, 0, 2, 3, 4)
    starts = jnp.arange(n_blocks, dtype=jnp.int32) * Q_BLOCK
    k_idx = jnp.arange(s, dtype=jnp.int32)

    def one_block(args):
        qb, start = args
        sc = jnp.einsum('bqhd,bkhd->bhqk', qb, k).astype(jnp.float32) * SOFTMAX_SCALE
        q_idx = start + jnp.arange(Q_BLOCK, dtype=jnp.int32)
        mask = q_idx[:, None] >= k_idx[None, :]
        p = jax.nn.softmax(jnp.where(mask, sc, -jnp.inf), axis=-1)
        return jnp.einsum('bhqk,bkhd->bqhd', p.astype(v.dtype), v)

    o = lax.map(one_block, (q_blocks, starts))
    return o.transpose(1, 0, 2, 3, 4).reshape(b, s, MLA_HEADS * V_HEAD_DIM)


def rg_lru(x, w_rg, b_rg, w_ig, b_ig, lam):
    b, s, w = x.shape
    xh = x.reshape(b, s, LRU_BLOCKS, LRU_BLOCK_DIM)
    r = jax.nn.sigmoid(jnp.einsum('bshi,hij->bshj', xh, w_rg) + b_rg).reshape(b, s, w)
    i = jax.nn.sigmoid(jnp.einsum('bshi,hij->bshj', xh, w_ig) + b_ig).reshape(b, s, w)
    log_a = -LRU_C * r.astype(jnp.float32) * jax.nn.softplus(-lam.astype(jnp.float32))
    a = jnp.exp(log_a)
    u = jnp.sqrt(-jnp.expm1(2.0 * log_a)) * (i * x).astype(jnp.float32)

    def combine(left, right):
        a1, b1 = left
        a2, b2 = right
        return a1 * a2, a2 * b1 + b2

    _, h = lax.associative_scan(combine, (a, u), axis=1)
    return h.astype(x.dtype)


def setup_inputs(seed: int = 0) -> dict:
    key = jax.random.key(seed)
    ks = jax.random.split(key, 32)
    f32 = jnp.float32

    def nrm(k, shape, fan_in):
        return jax.random.normal(k, shape, f32) * (fan_in ** -0.5)

    def gain(k, shape):
        return 1.0 + 0.05 * jax.random.normal(k, shape, f32)

    def bias(k, shape):
        return 0.01 * jax.random.normal(k, shape, f32)

    L = DEPTH
    x = jax.random.normal(ks[0], (BATCH, SEQ, D_MODEL), f32)
    mem = jax.random.normal(ks[1], (BATCH, N_MEM, D_MODEL), f32)
    offset = jax.random.randint(ks[2], (BATCH, 1), 0, 1024, dtype=jnp.int32)
    positions = offset + jnp.arange(SEQ, dtype=jnp.int32)[None, :]
    a0 = jax.random.uniform(ks[3], (L, LRU_WIDTH), f32, 0.9, 0.999)
    sig = a0 ** (1.0 / LRU_C)
    lru_lambda = jnp.log(sig) - jnp.log1p(-sig)
    return {
        "x": x,
        "mem": mem,
        "positions": positions,
        "g_pre_mix": gain(ks[4], (L, D_MODEL)),
        "g_post_mix": gain(ks[5], (L, D_MODEL)),
        "w_in": nrm(ks[6], (L, D_MODEL, IN_COLS), D_MODEL),
        "g_ckv": gain(ks[7], (L, KV_LORA_RANK)),
        "w_ukv": nrm(ks[8], (L, KV_LORA_RANK, MLA_HEADS * (QK_NOPE_DIM + V_HEAD_DIM)), KV_LORA_RANK),
        "w_o_mla": nrm(ks[9], (L, MLA_HEADS * V_HEAD_DIM, D_MODEL), MLA_HEADS * V_HEAD_DIM),
        "w_conv_lru": nrm(ks[10], (L, LRU_CONV_WIDTH, LRU_WIDTH), LRU_CONV_WIDTH),
        "b_conv_lru": bias(ks[11], (L, LRU_WIDTH)),
        "w_rg": nrm(ks[12], (L, LRU_BLOCKS, LRU_BLOCK_DIM, LRU_BLOCK_DIM), LRU_BLOCK_DIM),
        "b_rg": bias(ks[13], (L, LRU_BLOCKS, LRU_BLOCK_DIM)),
        "w_ig": nrm(ks[14], (L, LRU_BLOCKS, LRU_BLOCK_DIM, LRU_BLOCK_DIM), LRU_BLOCK_DIM),
        "b_ig": bias(ks[15], (L, LRU_BLOCKS, LRU_BLOCK_DIM)),
        "lru_lambda": lru_lambda,
        "w_o_lru": nrm(ks[16], (L, LRU_WIDTH, D_MODEL), LRU_WIDTH),
        "w_out": nrm(ks[17], (L, D_MODEL, D_MODEL), D_MODEL),
        "g_pre_x": gain(ks[18], (L, D_MODEL)),
        "g_post_x": gain(ks[19], (L, D_MODEL)),
        "g_mem": gain(ks[20], (L, D_MODEL)),
        "w_cq": nrm(ks[21], (L, D_MODEL, X_HEADS * X_HEAD_DIM), D_MODEL),
        "w_ck": nrm(ks[22], (L, D_MODEL, X_HEADS * X_HEAD_DIM), D_MODEL),
        "w_cv": nrm(ks[23], (L, D_MODEL, X_HEADS * X_HEAD_DIM), D_MODEL),
        "w_co": nrm(ks[24], (L, X_HEADS * X_HEAD_DIM, D_MODEL), X_HEADS * X_HEAD_DIM),
        "g_pre_ffn": gain(ks[25], (L, D_MODEL)),
        "g_post_ffn": gain(ks[26], (L, D_MODEL)),
        "w_up": nrm(ks[27], (L, D_MODEL, 2 * D_FF), D_MODEL),
        "w_fconv": nrm(ks[28], (L, FFN_CONV_WIDTH, 2 * D_FF), FFN_CONV_WIDTH),
        "b_fconv": bias(ks[29], (L, 2 * D_FF)),
        "w_down": nrm(ks[30], (L, D_FF, D_MODEL), D_FF),
    }


def reference(x, mem, positions, g_pre_mix, g_post_mix, w_in, g_ckv, w_ukv, w_o_mla,
              w_conv_lru, b_conv_lru, w_rg, b_rg, w_ig, b_ig, lru_lambda, w_o_lru, w_out,
              g_pre_x, g_post_x, g_mem, w_cq, w_ck, w_cv, w_co,
              g_pre_ffn, g_post_ffn, w_up, w_fconv, b_fconv, w_down):
    b, s, _ = x.shape
    m = mem.shape[1]
    for l in range(DEPTH):
        h = rmsnorm(x, g_pre_mix[l])
        proj = h @ w_in[l]
        q_in, ckv, k_rope, lru_x, lru_y, gate_mla, gate_lru = jnp.split(proj, IN_SPLIT_IDX, axis=-1)
        o_mla = mla_attention(q_in, ckv, k_rope, positions, g_ckv[l], w_ukv[l]) @ w_o_mla[l]
        xc = causal_dwconv(lru_x, w_conv_lru[l], b_conv_lru[l])
        hr = rg_lru(xc, w_rg[l], b_rg[l], w_ig[l], b_ig[l], lru_lambda[l])
        o_lru = (hr * jax.nn.gelu(lru_y, approximate=True)) @ w_o_lru[l]
        merged = jax.nn.sigmoid(gate_mla) * o_mla + jax.nn.sigmoid(gate_lru) * o_lru
        x = x + rmsnorm(merged @ w_out[l], g_post_mix[l])
        h = rmsnorm(x, g_pre_x[l])
        mn = rmsnorm(mem, g_mem[l])
        cq = (h @ w_cq[l]).reshape(b, s, X_HEADS, X_HEAD_DIM)
        ck = (mn @ w_ck[l]).reshape(b, m, X_HEADS, X_HEAD_DIM)
        cv = (mn @ w_cv[l]).reshape(b, m, X_HEADS, X_HEAD_DIM)
        sc = jnp.einsum('bshd,bmhd->bhsm', cq, ck).astype(jnp.float32) * (X_HEAD_DIM ** -0.5)
        p = jax.nn.softmax(sc, axis=-1).astype(cv.dtype)
        co = jnp.einsum('bhsm,bmhd->bshd', p, cv).reshape(b, s, X_HEADS * X_HEAD_DIM)
        x = x + rmsnorm(co @ w_co[l], g_post_x[l])
        h = rmsnorm(x, g_pre_ffn[l])
        up = causal_dwconv(h @ w_up[l], w_fconv[l], b_fconv[l])
        gate, val = jnp.split(up, 2, axis=-1)
        x = x + rmsnorm((jax.nn.gelu(gate, approximate=True) * val) @ w_down[l], g_post_ffn[l])
    return x
```

```python
import functools
import math

import jax
import jax.numpy as jnp
from jax import lax
from jax.experimental import pallas as pl
from jax.experimental.pallas import tpu as pltpu

F32 = jnp.float32
BF16 = jnp.bfloat16

D_MODEL = 2048
MLA_HEADS = 16
QK_NOPE_DIM = 128
QK_ROPE_DIM = 64
V_HEAD_DIM = 128
KV_LORA_RANK = 512
QK_HEAD_DIM = QK_NOPE_DIM + QK_ROPE_DIM
ROPE_THETA = 10000.0
LRU_WIDTH = D_MODEL
LRU_BLOCKS = 16
LRU_BLOCK_DIM = LRU_WIDTH // LRU_BLOCKS
LRU_CONV_WIDTH = 4
LRU_C = 8.0
X_HEADS = 4
X_HEAD_DIM = 128
D_FF = 5632
FFN_CONV_WIDTH = 3
EPS = 1e-6

LOG2E = math.log2(math.e)
LANES = 128
Q_PAD_DIM = 2 * LANES
Q_COLS = MLA_HEADS * QK_HEAD_DIM
Q_PAD_COLS = MLA_HEADS * Q_PAD_DIM
VMEM_LIMIT = 56 * 1024 * 1024

TM_PROJ, TN_PROJ = 512, 1024
TM_KV = 512
TQ, TK = 512, 512
TS_LRU, CB_LRU = 512, 256
TM_MERGE = 256
TM_CROSS = 512
TM_UP, TN_UP, HALO_UP = 1024, 512, 16
TM_DOWN = 256


def _params(*sem):
    return pltpu.CompilerParams(dimension_semantics=sem, vmem_limit_bytes=VMEM_LIMIT)


def _resident(shape, index_map):
    return pl.BlockSpec(shape, index_map, pipeline_mode=pl.Buffered(1))


def _rms(x, g):
    return x * lax.rsqrt(jnp.mean(x * x, axis=-1, keepdims=True) + EPS) * g


def _gelu_tanh(x):
    c = math.sqrt(2.0 / math.pi)
    return 0.5 * x * (1.0 + jnp.tanh(c * (x + 0.044715 * (x * x * x))))


def _rope_slab(slab, trig_ref):
    c = trig_ref[:, 0:LANES]
    s1 = trig_ref[:, LANES:2 * LANES]
    s2 = trig_ref[:, 2 * LANES:3 * LANES]
    return slab * c + pltpu.roll(slab, 96, 1) * s1 + pltpu.roll(slab, 32, 1) * s2


def _proj_kernel(x_ref, g_ref, w_ref, trig_ref, o_ref, h_ref, *, n_q_tiles):
    j = pl.program_id(1)

    @pl.when(j == 0)
    def _():
        h_ref[...] = _rms(x_ref[...], g_ref[...]).astype(BF16)

    acc = jnp.dot(h_ref[...], w_ref[...], preferred_element_type=F32)

    @pl.when(j >= n_q_tiles)
    def _():
        o_ref[...] = acc.astype(BF16)

    @pl.when(j < n_q_tiles)
    def _():
        for hh in range(TN_PROJ // Q_PAD_DIM):
            c0 = hh * Q_PAD_DIM
            o_ref[:, c0:c0 + LANES] = acc[:, c0:c0 + LANES].astype(BF16)
            slab = acc[:, c0 + LANES:c0 + Q_PAD_DIM]
            o_ref[:, c0 + LANES:c0 + Q_PAD_DIM] = _rope_slab(slab, trig_ref).astype(BF16)


def _proj_in(x2d, g, w_a, trig):
    t = x2d.shape[0]
    n = w_a.shape[1]
    return pl.pallas_call(
        functools.partial(_proj_kernel, n_q_tiles=Q_PAD_COLS // TN_PROJ),
        grid=(t // TM_PROJ, n // TN_PROJ),
        in_specs=[
            pl.BlockSpec((TM_PROJ, D_MODEL), lambda i, j: (i, 0)),
            pl.BlockSpec((1, D_MODEL), lambda i, j: (0, 0)),
            pl.BlockSpec((D_MODEL, TN_PROJ), lambda i, j: (0, j)),
            pl.BlockSpec((TM_PROJ, 3 * LANES), lambda i, j: (i, 0)),
        ],
        out_specs=pl.BlockSpec((TM_PROJ, TN_PROJ), lambda i, j: (i, j)),
        out_shape=jax.ShapeDtypeStruct((t, n), BF16),
        scratch_shapes=[pltpu.VMEM((TM_PROJ, D_MODEL), BF16)],
        compiler_params=_params("arbitrary", "arbitrary"),
        name="proj_in",
    )(x2d, g, w_a, trig)


def _kv_kernel(x_ref, g_ref, wb_ref, gc_ref, wuk_ref, wuvt_ref, trig_ref,
               kn_ref, kr_ref, vt_ref):
    h = _rms(x_ref[...], g_ref[...]).astype(BF16)
    c = jnp.dot(h, wb_ref[...], preferred_element_type=F32)
    n = _rms(c[:, :KV_LORA_RANK], gc_ref[...]).astype(BF16)
    kn_ref[...] = jnp.dot(n, wuk_ref[...], preferred_element_type=F32).astype(BF16)
    vt_ref[...] = lax.dot_general(wuvt_ref[...], n, (((1,), (1,)), ((), ())),
                                  preferred_element_type=F32).astype(BF16)
    kr_ref[...] = _rope_slab(c[:, KV_LORA_RANK:], trig_ref).astype(BF16)


def _kv_path(x2d, g, w_b, g_ckv, w_uk, w_uvt, trig, batch, seq):
    t = x2d.shape[0]
    nk = seq // TM_KV
    hv = MLA_HEADS * V_HEAD_DIM
    return pl.pallas_call(
        _kv_kernel,
        grid=(t // TM_KV,),
        in_specs=[
            pl.BlockSpec((TM_KV, D_MODEL), lambda i: (i, 0)),
            pl.BlockSpec((1, D_MODEL), lambda i: (0, 0)),
            _resident(w_b.shape, lambda i: (0, 0)),
            pl.BlockSpec((1, KV_LORA_RANK), lambda i: (0, 0)),
            _resident(w_uk.shape, lambda i: (0, 0)),
            _resident(w_uvt.shape, lambda i: (0, 0)),
            pl.BlockSpec((TM_KV, 3 * LANES), lambda i: (i, 0)),
        ],
        out_specs=[
            pl.BlockSpec((TM_KV, hv), lambda i: (i, 0)),
            pl.BlockSpec((TM_KV, LANES), lambda i: (i, 0)),
            pl.BlockSpec((None, hv, TM_KV), lambda i: (i, 0, 0)),
        ],
        out_shape=[
            jax.ShapeDtypeStruct((t, hv), BF16),
            jax.ShapeDtypeStruct((t, LANES), BF16),
            jax.ShapeDtypeStruct((batch * nk, hv, TM_KV), BF16),
        ],
        compiler_params=_params("arbitrary"),
        name="kv_path",
    )(x2d, g, w_b, g_ckv, w_uk, w_uvt, trig)


def _attn_kernel(q_ref, kn_ref, kr_ref, vt_ref, o_ref,
                 kcat_ref, qt_ref, acc_ref, m_ref, l_ref):
    i = pl.program_id(2)

    @pl.when(i == 0)
    def _():
        kcat_ref[:, 0:LANES] = kn_ref[...]
        kcat_ref[:, LANES:Q_PAD_DIM] = kr_ref[...]

    qt_ref[...] = q_ref[...].astype(F32).T.astype(BF16)
    m_ref[...] = jnp.full(m_ref.shape, -jnp.inf, F32)
    l_ref[...] = jnp.zeros(l_ref.shape, F32)
    acc_ref[...] = jnp.zeros(acc_ref.shape, F32)

    def block(j, masked):
        k0 = pl.multiple_of(j * TK, TK)
        s = jnp.dot(kcat_ref[pl.ds(k0, TK), :], qt_ref[...],
                    preferred_element_type=F32)
        if masked:
            key = lax.broadcasted_iota(jnp.int32, (TK, TQ), 0) + k0
            qry = lax.broadcasted_iota(jnp.int32, (TK, TQ), 1) + i * TQ
            s = jnp.where(key <= qry, s, -jnp.inf)
        m_old = m_ref[...]
        m_new = jnp.maximum(m_old, jnp.max(s, axis=0, keepdims=True))
        p = jnp.exp2(s - m_new)
        alpha = jnp.exp2(m_old - m_new)
        l_ref[...] = alpha * l_ref[...] + jnp.sum(p, axis=0, keepdims=True)
        acc_ref[...] = acc_ref[...] * alpha + jnp.dot(
            vt_ref[j], p.astype(BF16), preferred_element_type=F32)
        m_ref[...] = m_new

    n_full = i * (TQ // TK)

    def full_block(j, carry):
        block(j, False)
        return carry

    lax.fori_loop(0, n_full, full_block, 0)
    for d in range(TQ // TK):
        block(n_full + d, True)

    out = acc_ref[...] * (1.0 / l_ref[...])
    o_ref[...] = out.T.astype(BF16)


def _mla_attn(proj3, kn3, kr3, vt4, batch, seq):
    nk = seq // TK
    hv = MLA_HEADS * V_HEAD_DIM
    return pl.pallas_call(
        _attn_kernel,
        grid=(batch, MLA_HEADS, seq // TQ),
        in_specs=[
            pl.BlockSpec((None, TQ, Q_PAD_DIM), lambda b, h, i: (b, i, h)),
            pl.BlockSpec((None, seq, LANES), lambda b, h, i: (b, 0, h)),
            pl.BlockSpec((None, seq, LANES), lambda b, h, i: (b, 0, 0)),
            pl.BlockSpec((None, nk, V_HEAD_DIM, TK), lambda b, h, i: (b, 0, h, 0)),
        ],
        out_specs=pl.BlockSpec((None, TQ, V_HEAD_DIM), lambda b, h, i: (b, i, h)),
        out_shape=jax.ShapeDtypeStruct((batch, seq, hv), BF16),
        scratch_shapes=[
            pltpu.VMEM((seq, Q_PAD_DIM), BF16),
            pltpu.VMEM((Q_PAD_DIM, TQ), BF16),
            pltpu.VMEM((V_HEAD_DIM, TQ), F32),
            pltpu.VMEM((1, TQ), F32),
            pltpu.VMEM((1, TQ), F32),
        ],
        compiler_params=_params("arbitrary", "arbitrary", "arbitrary"),
        name="mla_attn",
    )(proj3, kn3, kr3, vt4)


def _lru_kernel(x_ref, y_ref, wc_ref, bc_ref, wrg_ref, brg_ref, wig_ref, big_ref, lam_ref,
                o_ref, xbuf, a_s, b_s, hc_ref):
    si = pl.program_id(2)
    ts, cb = TS_LRU, CB_LRU

    @pl.when(si == 0)
    def _():
        xbuf[0:8, :] = jnp.zeros((8, cb), F32)
        hc_ref[...] = jnp.zeros((8, cb), F32)

    xbuf[8:8 + ts, :] = x_ref[...].astype(F32)
    xc = bc_ref[...]
    for k in range(LRU_CONV_WIDTH):
        xc = xc + xbuf[pl.ds(8 - (LRU_CONV_WIDTH - 1) + k, ts), :] * wc_ref[k:k + 1, :]
    xbuf[0:8, :] = xbuf[ts:ts + 8, :]

    xcb = xc.astype(BF16)
    r = jax.nn.sigmoid(jnp.dot(xcb, wrg_ref[...], preferred_element_type=F32) + brg_ref[...])
    ig = jax.nn.sigmoid(jnp.dot(xcb, wig_ref[...], preferred_element_type=F32) + big_ref[...])
    z = -lam_ref[...]
    softplus = jnp.maximum(z, 0.0) + jnp.log1p(jnp.exp(-jnp.abs(z)))
    a = jnp.exp((-LRU_C * softplus) * r)
    u = jnp.sqrt(1.0 - a * a) * (ig * xc)

    row = lax.broadcasted_iota(jnp.int32, (ts, cb), 0) & 7
    for d in (1, 2, 4):
        keep = row >= d
        a_sh = jnp.where(keep, pltpu.roll(a, d, 0), 1.0)
        u_sh = jnp.where(keep, pltpu.roll(u, d, 0), 0.0)
        u = a * u_sh + u
        a = a * a_sh
    a_s[...] = a
    b_s[...] = u

    hc = hc_ref[...]
    for g in range(ts // 8):
        hg = a_s[8 * g:8 * g + 8, :] * hc + b_s[8 * g:8 * g + 8, :]
        b_s[8 * g:8 * g + 8, :] = hg
        hc = jnp.broadcast_to(hg[7:8, :], (8, cb))
    hc_ref[...] = hc

    o_ref[...] = (b_s[...] * _gelu_tanh(y_ref[...].astype(F32))).astype(BF16)


def _lru(proj3, w_conv, b_conv, w_rg2, b_rg, w_ig2, b_ig, lam, batch, seq):
    ts, cb = TS_LRU, CB_LRU
    ncb = LRU_WIDTH // cb
    x_off = Q_PAD_COLS // cb
    y_off = x_off + ncb
    return pl.pallas_call(
        _lru_kernel,
        grid=(batch, ncb, seq // ts),
        in_specs=[
            pl.BlockSpec((None, ts, cb), lambda b, c, s: (b, s, x_off + c)),
            pl.BlockSpec((None, ts, cb), lambda b, c, s: (b, s, y_off + c)),
            pl.BlockSpec((LRU_CONV_WIDTH, cb), lambda b, c, s: (0, c)),
            pl.BlockSpec((1, cb), lambda b, c, s: (0, c)),
            pl.BlockSpec((None, cb, cb), lambda b, c, s: (c, 0, 0)),
            pl.BlockSpec((1, cb), lambda b, c, s: (0, c)),
            pl.BlockSpec((None, cb, cb), lambda b, c, s: (c, 0, 0)),
            pl.BlockSpec((1, cb), lambda b, c, s: (0, c)),
            pl.BlockSpec((1, cb), lambda b, c, s: (0, c)),
        ],
        out_specs=pl.BlockSpec((None, ts, cb), lambda b, c, s: (b, s, c)),
        out_shape=jax.ShapeDtypeStruct((batch, seq, LRU_WIDTH), BF16),
        scratch_shapes=[
            pltpu.VMEM((ts + 8, cb), F32),
            pltpu.VMEM((ts, cb), F32),
            pltpu.VMEM((ts, cb), F32),
            pltpu.VMEM((8, cb), F32),
        ],
        compiler_params=_params("arbitrary", "arbitrary", "arbitrary"),
        name="lru",
    )(proj3, proj3, w_conv, b_conv, w_rg2, b_rg, w_ig2, b_ig, lam)


def _merge_kernel(attn_ref, lru_ref, gm_ref, gl_ref, x_ref, wom_ref, wol_ref, wout_ref, g_ref,
                  o_ref):
    o_mla = jnp.dot(attn_ref[...], wom_ref[...], preferred_element_type=F32)
    o_lru = jnp.dot(lru_ref[...], wol_ref[...], preferred_element_type=F32)
    merged = (jax.nn.sigmoid(gm_ref[...].astype(F32)) * o_mla
              + jax.nn.sigmoid(gl_ref[...].astype(F32)) * o_lru)
    y = jnp.dot(merged.astype(BF16), wout_ref[...], preferred_element_type=F32)
    o_ref[...] = x_ref[...] + _rms(y, g_ref[...])


def _merge(attn2, lru2, proj2, x2d, w_om, w_ol, w_out, g_post):
    t = x2d.shape[0]
    tm = TM_MERGE
    gm_off = (Q_PAD_COLS + 2 * LRU_WIDTH) // D_MODEL
    row = lambda i: (i, 0)
    const = lambda i: (0, 0)
    return pl.pallas_call(
        _merge_kernel,
        grid=(t // tm,),
        in_specs=[
            pl.BlockSpec((tm, D_MODEL), row),
            pl.BlockSpec((tm, D_MODEL), row),
            pl.BlockSpec((tm, D_MODEL), lambda i: (i, gm_off)),
            pl.BlockSpec((tm, D_MODEL), lambda i: (i, gm_off + 1)),
            pl.BlockSpec((tm, D_MODEL), row),
            _resident((D_MODEL, D_MODEL), const),
            _resident((D_MODEL, D_MODEL), const),
            _resident((D_MODEL, D_MODEL), const),
            pl.BlockSpec((1, D_MODEL), const),
        ],
        out_specs=pl.BlockSpec((tm, D_MODEL), row),
        out_shape=jax.ShapeDtypeStruct((t, D_MODEL), F32),
        compiler_params=_params("arbitrary"),
        name="merge",
    )(attn2, lru2, proj2, proj2, x2d, w_om, w_ol, w_out, g_post)


def _memkv_kernel(mem_ref, g_ref, wckt_ref, wcv_ref, ckt_ref, cv_ref):
    mn = _rms(mem_ref[...], g_ref[...]).astype(BF16)
    ckt_ref[...] = lax.dot_general(wckt_ref[...], mn, (((1,), (1,)), ((), ())),
                                   preferred_element_type=F32).astype(BF16)
    cv_ref[...] = jnp.dot(mn, wcv_ref[...], preferred_element_type=F32).astype(BF16)


def _mem_kv(mem, g_mem, w_ckt, w_cv):
    b, m, _ = mem.shape
    xd = X_HEADS * X_HEAD_DIM
    return pl.pallas_call(
        _memkv_kernel,
        grid=(b,),
        in_specs=[
            pl.BlockSpec((None, m, D_MODEL), lambda i: (i, 0, 0)),
            pl.BlockSpec((1, D_MODEL), lambda i: (0, 0)),
            pl.BlockSpec((xd, D_MODEL), lambda i: (0, 0)),
            pl.BlockSpec((D_MODEL, xd), lambda i: (0, 0)),
        ],
        out_specs=[
            pl.BlockSpec((None, xd, m), lambda i: (i, 0, 0)),
            pl.BlockSpec((None, m, xd), lambda i: (i, 0, 0)),
        ],
        out_shape=[
            jax.ShapeDtypeStruct((b, xd, m), BF16),
            jax.ShapeDtypeStruct((b, m, xd), BF16),
        ],
        compiler_params=_params("arbitrary"),
        name="mem_kv",
    )(mem, g_mem, w_ckt, w_cv)


def _cross_kernel(x_ref, gpre_ref, wcq_ref, ckt_ref, cv_ref, wco_ref, gpost_ref, o_ref):
    x = x_ref[...]
    h = _rms(x, gpre_ref[...]).astype(BF16)
    cq = jnp.dot(h, wcq_ref[...], preferred_element_type=F32).astype(BF16)
    outs = []
    for hh in range(X_HEADS):
        sl = slice(hh * X_HEAD_DIM, (hh + 1) * X_HEAD_DIM)
        s = jnp.dot(cq[:, sl], ckt_ref[sl, :], preferred_element_type=F32)
        p = jnp.exp2(s - jnp.max(s, axis=-1, keepdims=True))
        l = jnp.sum(p, axis=-1, keepdims=True)
        pv = jnp.dot(p.astype(BF16), cv_ref[:, sl], preferred_element_type=F32)
        outs.append((pv * (1.0 / l)).astype(BF16))
    co = jnp.concatenate(outs, axis=-1)
    y = jnp.dot(co, wco_ref[...], preferred_element_type=F32)
    o_ref[...] = x + _rms(y, gpost_ref[...])


def _cross(x2d, g_pre, w_cq, ckt, cv, w_co, g_post, seq):
    t = x2d.shape[0]
    tm = TM_CROSS
    xd = X_HEADS * X_HEAD_DIM
    m = cv.shape[1]
    per_b = seq // tm
    return pl.pallas_call(
        _cross_kernel,
        grid=(t // tm,),
        in_specs=[
            pl.BlockSpec((tm, D_MODEL), lambda i: (i, 0)),
            pl.BlockSpec((1, D_MODEL), lambda i: (0, 0)),
            pl.BlockSpec((D_MODEL, xd), lambda i: (0, 0)),
            pl.BlockSpec((None, xd, m), lambda i: (i // per_b, 0, 0)),
            pl.BlockSpec((None, m, xd), lambda i: (i // per_b, 0, 0)),
            pl.BlockSpec((xd, D_MODEL), lambda i: (0, 0)),
            pl.BlockSpec((1, D_MODEL), lambda i: (0, 0)),
        ],
        out_specs=pl.BlockSpec((tm, D_MODEL), lambda i: (i, 0)),
        out_shape=jax.ShapeDtypeStruct((t, D_MODEL), F32),
        compiler_params=_params("arbitrary"),
        name="cross",
    )(x2d, g_pre, w_cq, ckt, cv, w_co, g_post)


def _up_kernel(x_ref, halo_ref, g_ref, wg_ref, wv_ref, cwg_ref, cwv_ref, cbg_ref, cbv_ref,
               o_ref, h_ref, ag_ref, av_ref, *, tiles_per_seq):
    i = pl.program_id(0)
    j = pl.program_id(1)
    tm, halo = TM_UP, HALO_UP

    @pl.when(j == 0)
    def _():
        g = g_ref[...]
        keep = jnp.where(i % tiles_per_seq == 0, 0.0, 1.0)
        h_ref[0:halo, :] = (_rms(halo_ref[...], g) * keep).astype(BF16)
        h_ref[halo:halo + tm, :] = _rms(x_ref[...], g).astype(BF16)

    h = h_ref[...]
    ag_ref[...] = jnp.dot(h, wg_ref[...], preferred_element_type=F32)
    av_ref[...] = jnp.dot(h, wv_ref[...], preferred_element_type=F32)

    def conv(a_ref, w_ref, b_ref):
        acc = b_ref[...]
        for k in range(FFN_CONV_WIDTH):
            acc = acc + a_ref[pl.ds(halo - (FFN_CONV_WIDTH - 1) + k, tm), :] * w_ref[k:k + 1, :]
        return acc

    gate = conv(ag_ref, cwg_ref, cbg_ref)
    val = conv(av_ref, cwv_ref, cbv_ref)
    o_ref[...] = (_gelu_tanh(gate) * val).astype(BF16)


def _ffn_up(x2d, g_pre, w_up, w_fconv, b_fconv, seq):
    t = x2d.shape[0]
    tm, tn, halo = TM_UP, TN_UP, HALO_UP
    nj = D_FF // tn
    return pl.pallas_call(
        functools.partial(_up_kernel, tiles_per_seq=seq // tm),
        grid=(t // tm, nj),
        in_specs=[
            pl.BlockSpec((tm, D_MODEL), lambda i, j: (i, 0)),
            pl.BlockSpec((halo, D_MODEL), lambda i, j: (jnp.maximum(i * (tm // halo) - 1, 0), 0)),
            pl.BlockSpec((1, D_MODEL), lambda i, j: (0, 0)),
            pl.BlockSpec((D_MODEL, tn), lambda i, j: (0, j)),
            pl.BlockSpec((D_MODEL, tn), lambda i, j: (0, j + nj)),
            pl.BlockSpec((FFN_CONV_WIDTH, tn), lambda i, j: (0, j)),
            pl.BlockSpec((FFN_CONV_WIDTH, tn), lambda i, j: (0, j + nj)),
            pl.BlockSpec((1, tn), lambda i, j: (0, j)),
            pl.BlockSpec((1, tn), lambda i, j: (0, j + nj)),
        ],
        out_specs=pl.BlockSpec((tm, tn), lambda i, j: (i, j)),
        out_shape=jax.ShapeDtypeStruct((t, D_FF), BF16),
        scratch_shapes=[
            pltpu.VMEM((tm + halo, D_MODEL), BF16),
            pltpu.VMEM((tm + halo, tn), F32),
            pltpu.VMEM((tm + halo, tn), F32),
        ],
        compiler_params=_params("arbitrary", "arbitrary"),
        name="ffn_up",
    )(x2d, x2d, g_pre, w_up, w_up, w_fconv, w_fconv, b_fconv, b_fconv)


def _down_kernel(a_ref, x_ref, w_ref, g_ref, o_ref):
    y = jnp.dot(a_ref[...], w_ref[...], preferred_element_type=F32)
    o_ref[...] = x_ref[...] + _rms(y, g_ref[...])


def _ffn_down(act, x2d, w_down, g_post):
    t = x2d.shape[0]
    tm = TM_DOWN
    return pl.pallas_call(
        _down_kernel,
        grid=(t // tm,),
        in_specs=[
            pl.BlockSpec((tm, D_FF), lambda i: (i, 0)),
            pl.BlockSpec((tm, D_MODEL), lambda i: (i, 0)),
            _resident((D_FF, D_MODEL), lambda i: (0, 0)),
            pl.BlockSpec((1, D_MODEL), lambda i: (0, 0)),
        ],
        out_specs=pl.BlockSpec((tm, D_MODEL), lambda i: (i, 0)),
        out_shape=jax.ShapeDtypeStruct((t, D_MODEL), F32),
        compiler_params=_params("arbitrary"),
        name="ffn_down",
    )(act, x2d, w_down, g_post)


def _rope_table(positions):
    half = QK_ROPE_DIM // 2
    inv_freq = ROPE_THETA ** (-jnp.arange(0, QK_ROPE_DIM, 2, dtype=F32) / QK_ROPE_DIM)
    ang = positions.astype(F32).reshape(-1, 1) * inv_freq
    cos, sin = jnp.cos(ang), jnp.sin(ang)
    z = jnp.zeros_like(cos)
    assert half * 4 == LANES
    return jnp.concatenate([cos, cos, z, z, -sin, z, z, z, z, sin, z, z], axis=-1)


def _block_diag_pairs(w):
    nb, d, _ = w.shape
    w = w.reshape(nb // 2, 2, d, d)
    z = jnp.zeros_like(w[:, 0])
    top = jnp.concatenate([w[:, 0], z], axis=-1)
    bot = jnp.concatenate([z, w[:, 1]], axis=-1)
    return jnp.concatenate([top, bot], axis=-2)


def kernel(x, mem, positions, g_pre_mix, g_post_mix, w_in, g_ckv, w_ukv, w_o_mla, w_conv_lru, b_conv_lru, w_rg, b_rg, w_ig, b_ig, lru_lambda, w_o_lru, w_out, g_pre_x, g_post_x, g_mem, w_cq, w_ck, w_cv, w_co, g_pre_ffn, g_post_ffn, w_up, w_fconv, b_fconv, w_down):
    batch, seq, d = x.shape
    assert d == D_MODEL and g_pre_mix.shape[0] == 1
    t = batch * seq
    x2d = x.reshape(t, d)
    l = 0

    q_scale = (QK_HEAD_DIM ** -0.5) * LOG2E
    wi = w_in[l]
    wq = wi[:, :Q_COLS].reshape(d, MLA_HEADS, QK_HEAD_DIM) * q_scale
    wq = jnp.pad(wq, ((0, 0), (0, 0), (0, Q_PAD_DIM - QK_HEAD_DIM))).reshape(d, Q_PAD_COLS)
    kv_end = Q_COLS + KV_LORA_RANK + QK_ROPE_DIM
    w_a = jnp.concatenate([wq, wi[:, kv_end:]], axis=1).astype(BF16)
    w_b = jnp.pad(wi[:, Q_COLS:kv_end], ((0, 0), (0, LANES - QK_ROPE_DIM))).astype(BF16)
    wukv = w_ukv[l].reshape(KV_LORA_RANK, MLA_HEADS, QK_NOPE_DIM + V_HEAD_DIM)
    w_uk = wukv[:, :, :QK_NOPE_DIM].reshape(KV_LORA_RANK, -1).astype(BF16)
    w_uvt = wukv[:, :, QK_NOPE_DIM:].reshape(KV_LORA_RANK, -1).T.astype(BF16)
    row = lambda v: v.reshape(1, -1)
    trig = _rope_table(positions)

    proj = _proj_in(x2d, row(g_pre_mix[l]), w_a, trig)
    kn, kr, vt = _kv_path(x2d, row(g_pre_mix[l]), w_b, row(g_ckv[l]), w_uk, w_uvt, trig,
                          batch, seq)
    proj3 = proj.reshape(batch, seq, -1)
    attn = _mla_attn(proj3, kn.reshape(batch, seq, -1), kr.reshape(batch, seq, LANES),
                     vt.reshape(batch, seq // TK, MLA_HEADS * V_HEAD_DIM, TK), batch, seq)
    lru = _lru(proj3, w_conv_lru[l], row(b_conv_lru[l]),
               _block_diag_pairs(w_rg[l]).astype(BF16), row(b_rg[l]),
               _block_diag_pairs(w_ig[l]).astype(BF16), row(b_ig[l]),
               row(lru_lambda[l]), batch, seq)
    x1 = _merge(attn.reshape(t, -1), lru.reshape(t, -1), proj, x2d,
                w_o_mla[l].astype(BF16), w_o_lru[l].astype(BF16), w_out[l].astype(BF16),
                row(g_post_mix[l]))

    c_scale = (X_HEAD_DIM ** -0.5) * LOG2E
    ckt, cv = _mem_kv(mem, row(g_mem[l]), w_ck[l].T.astype(BF16), w_cv[l].astype(BF16))
    x2 = _cross(x1, row(g_pre_x[l]), (w_cq[l] * c_scale).astype(BF16), ckt, cv,
                w_co[l].astype(BF16), row(g_post_x[l]), seq)

    act = _ffn_up(x2, row(g_pre_ffn[l]), w_up[l].astype(BF16), w_fconv[l], row(b_fconv[l]), seq)
    x3 = _ffn_down(act, x2, w_down[l].astype(BF16), row(g_post_ffn[l]))
    return x3.reshape(batch, seq, d)
```

```python
import functools
import math

import jax
import jax.numpy as jnp
from jax import lax
from jax.experimental import pallas as pl
from jax.experimental.pallas import tpu as pltpu

F32 = jnp.float32
BF16 = jnp.bfloat16

D_MODEL = 2048
MLA_HEADS = 16
QK_NOPE_DIM = 128
QK_ROPE_DIM = 64
V_HEAD_DIM = 128
KV_LORA_RANK = 512
QK_HEAD_DIM = QK_NOPE_DIM + QK_ROPE_DIM
ROPE_THETA = 10000.0
LRU_WIDTH = D_MODEL
LRU_BLOCKS = 16
LRU_BLOCK_DIM = LRU_WIDTH // LRU_BLOCKS
LRU_CONV_WIDTH = 4
LRU_C = 8.0
X_HEADS = 4
X_HEAD_DIM = 128
D_FF = 5632
FFN_CONV_WIDTH = 3
EPS = 1e-6

LOG2E = math.log2(math.e)
LANES = 128
BF16_ROWS = 16
Q_PAD_DIM = 2 * LANES
V_AUG_DIM = V_HEAD_DIM + BF16_ROWS
Q_COLS = MLA_HEADS * QK_HEAD_DIM
Q_PAD_COLS = MLA_HEADS * Q_PAD_DIM
VMEM_LIMIT = 56 * 1024 * 1024

TM_PROJ, TN_PROJ = 1024, 1024
TM_KV = 512
TQ, TK = 1024, 512
assert TQ == 2 * TK and TM_KV == TK
TS_LRU, CB_LRU = 512, 256
TM_MERGE = 256
TM_CROSS = 512
TM_UP, TN_UP, HALO_UP = 1024, 512, 16
TM_DOWN = 256


def _params(*sem):
    return pltpu.CompilerParams(dimension_semantics=sem, vmem_limit_bytes=VMEM_LIMIT)


def _resident(shape, index_map):
    return pl.BlockSpec(shape, index_map, pipeline_mode=pl.Buffered(1))


def _rms(x, g):
    return x * lax.rsqrt(jnp.mean(x * x, axis=-1, keepdims=True) + EPS) * g


def _gelu_tanh(x):
    c = math.sqrt(2.0 / math.pi)
    return 0.5 * x * (1.0 + jnp.tanh(c * (x + 0.044715 * (x * x * x))))


def _rope_slab(slab, trig_ref):
    c = trig_ref[:, 0:LANES]
    s1 = trig_ref[:, LANES:2 * LANES]
    s2 = trig_ref[:, 2 * LANES:3 * LANES]
    return slab * c + pltpu.roll(slab, 96, 1) * s1 + pltpu.roll(slab, 32, 1) * s2


def _proj_kernel(x_ref, g_ref, w_ref, trig_ref, o_ref, h_ref, *, n_q_tiles):
    j = pl.program_id(1)

    @pl.when(j == 0)
    def _():
        h_ref[...] = _rms(x_ref[...], g_ref[...]).astype(BF16)

    acc = jnp.dot(h_ref[...], w_ref[...], preferred_element_type=F32)

    @pl.when(j >= n_q_tiles)
    def _():
        o_ref[...] = acc.astype(BF16)

    @pl.when(j < n_q_tiles)
    def _():
        for hh in range(TN_PROJ // Q_PAD_DIM):
            c0 = hh * Q_PAD_DIM
            o_ref[:, c0:c0 + LANES] = acc[:, c0:c0 + LANES].astype(BF16)
            slab = acc[:, c0 + LANES:c0 + Q_PAD_DIM]
            o_ref[:, c0 + LANES:c0 + Q_PAD_DIM] = _rope_slab(slab, trig_ref).astype(BF16)


def _proj_in(x2d, g, w_a, trig):
    t = x2d.shape[0]
    n = w_a.shape[1]
    return pl.pallas_call(
        functools.partial(_proj_kernel, n_q_tiles=Q_PAD_COLS // TN_PROJ),
        grid=(t // TM_PROJ, n // TN_PROJ),
        in_specs=[
            pl.BlockSpec((TM_PROJ, D_MODEL), lambda i, j: (i, 0)),
            pl.BlockSpec((1, D_MODEL), lambda i, j: (0, 0)),
            pl.BlockSpec((D_MODEL, TN_PROJ), lambda i, j: (0, j)),
            pl.BlockSpec((TM_PROJ, 3 * LANES), lambda i, j: (i, 0)),
        ],
        out_specs=pl.BlockSpec((TM_PROJ, TN_PROJ), lambda i, j: (i, j)),
        out_shape=jax.ShapeDtypeStruct((t, n), BF16),
        scratch_shapes=[pltpu.VMEM((TM_PROJ, D_MODEL), BF16)],
        compiler_params=_params("arbitrary", "arbitrary"),
        name="proj_in",
    )(x2d, g, w_a, trig)


def _kv_kernel(x_ref, g_ref, wb_ref, gc_ref, wuk_ref, wuvt_ref, trig_ref,
               kn_ref, kr_ref, vt_ref):
    h = _rms(x_ref[...], g_ref[...]).astype(BF16)
    c = jnp.dot(h, wb_ref[...], preferred_element_type=F32)
    n = _rms(c[:, :KV_LORA_RANK], gc_ref[...]).astype(BF16)
    kn_ref[...] = jnp.dot(n, wuk_ref[...], preferred_element_type=F32).astype(BF16)
    vt_ref[...] = lax.dot_general(wuvt_ref[...], n, (((1,), (1,)), ((), ())),
                                  preferred_element_type=F32).astype(BF16)
    kr_ref[...] = _rope_slab(c[:, KV_LORA_RANK:], trig_ref).astype(BF16)


def _kv_path(x2d, g, w_b, g_ckv, w_uk, w_uvt, trig, batch, seq):
    t = x2d.shape[0]
    nk = seq // TM_KV
    hv = MLA_HEADS * V_HEAD_DIM
    return pl.pallas_call(
        _kv_kernel,
        grid=(t // TM_KV,),
        in_specs=[
            pl.BlockSpec((TM_KV, D_MODEL), lambda i: (i, 0)),
            pl.BlockSpec((1, D_MODEL), lambda i: (0, 0)),
            _resident(w_b.shape, lambda i: (0, 0)),
            pl.BlockSpec((1, KV_LORA_RANK), lambda i: (0, 0)),
            _resident(w_uk.shape, lambda i: (0, 0)),
            _resident(w_uvt.shape, lambda i: (0, 0)),
            pl.BlockSpec((TM_KV, 3 * LANES), lambda i: (i, 0)),
        ],
        out_specs=[
            pl.BlockSpec((TM_KV, hv), lambda i: (i, 0)),
            pl.BlockSpec((TM_KV, LANES), lambda i: (i, 0)),
            pl.BlockSpec((None, hv, TM_KV), lambda i: (i, 0, 0)),
        ],
        out_shape=[
            jax.ShapeDtypeStruct((t, hv), BF16),
            jax.ShapeDtypeStruct((t, LANES), BF16),
            jax.ShapeDtypeStruct((batch * nk, hv, TM_KV), BF16),
        ],
        compiler_params=_params("arbitrary"),
        name="kv_path",
    )(x2d, g, w_b, g_ckv, w_uk, w_uvt, trig)


def _attn_kernel(q_ref, kn_ref, kr_ref, vt_ref, o_ref,
                 kcat_ref, vaug_ref, qt_ref, s0, s1, x0, x1, p0, p1, a0, a1, acc_ref, m_ref):
    i = pl.program_id(2)
    n = (TQ // TK) * (i + 1)

    @pl.when(i == 0)
    def _():
        kcat_ref[:, 0:LANES] = kn_ref[...]
        kcat_ref[:, LANES:Q_PAD_DIM] = kr_ref[...]
        vaug_ref[:, 0:V_HEAD_DIM, :] = vt_ref[...]
        vaug_ref[:, V_HEAD_DIM:V_AUG_DIM, :] = jnp.ones(
            (vaug_ref.shape[0], V_AUG_DIM - V_HEAD_DIM, TK), BF16)

    qt_ref[...] = q_ref[...].astype(F32).T.astype(BF16)
    m_ref[...] = jnp.full(m_ref.shape, -jnp.inf, F32)
    acc_ref[...] = jnp.zeros(acc_ref.shape, F32)

    def qk(t, s_ref, x_ref):
        k0 = pl.multiple_of(t * TK, TK)
        s = jnp.dot(kcat_ref[pl.ds(k0, TK), :], qt_ref[...],
                    preferred_element_type=F32)
        s_ref[...] = s
        x_ref[...] = jnp.max(s, axis=0, keepdims=True)

    def sm(t, s_ref, x_ref, p_ref, a_ref, masked):
        s = s_ref[...]
        m_old = m_ref[...]
        if masked:
            key = lax.broadcasted_iota(jnp.int32, (TK, TQ), 0) + t * TK
            qry = lax.broadcasted_iota(jnp.int32, (TK, TQ), 1) + i * TQ
            s = jnp.where(key <= qry, s, -jnp.inf)
            m_new = jnp.maximum(m_old, jnp.max(s, axis=0, keepdims=True))
        else:
            m_new = jnp.maximum(m_old, x_ref[...])
        p = jnp.exp2(s - m_new)
        alpha = jnp.exp2(m_old - m_new)
        m_ref[...] = m_new
        a_ref[...] = alpha
        p_ref[...] = p.astype(BF16)

    def pv(t, p_ref, a_ref):
        acc_ref[...] = acc_ref[...] * a_ref[...] + jnp.dot(
            vaug_ref[t], p_ref[...], preferred_element_type=F32)

    qk(0, s0, x0)

    @pl.when(i >= 1)
    def _():
        qk(1, s1, x1)
        sm(0, s0, x0, p0, a0, False)
        qk(2, s0, x0)
        sm(1, s1, x1, p1, a1, False)
        pv(0, p0, a0)

        def pair(k, carry):
            t = 2 * k + 3
            qk(t, s1, x1)
            sm(t - 1, s0, x0, p0, a0, False)
            pv(t - 2, p1, a1)
            qk(t + 1, s0, x0)
            sm(t, s1, x1, p1, a1, False)
            pv(t - 1, p0, a0)
            return carry

        lax.fori_loop(0, i - 1, pair, 0)
        qk(n - 1, s1, x1)
        sm(n - 2, s0, x0, p0, a0, True)
        pv(n - 3, p1, a1)

    @pl.when(i == 0)
    def _():
        qk(1, s1, x1)
        sm(0, s0, x0, p0, a0, True)

    sm(n - 1, s1, x1, p1, a1, True)
    pv(n - 2, p0, a0)
    pv(n - 1, p1, a1)

    out = acc_ref[0:V_HEAD_DIM, :] * (1.0 / acc_ref[V_HEAD_DIM:V_HEAD_DIM + 1, :])
    o_ref[...] = out.T.astype(BF16)


def _mla_attn(proj3, kn3, kr3, vt4, batch, seq):
    nk = seq // TK
    hv = MLA_HEADS * V_HEAD_DIM
    return pl.pallas_call(
        _attn_kernel,
        grid=(batch, MLA_HEADS, seq // TQ),
        in_specs=[
            pl.BlockSpec((None, TQ, Q_PAD_DIM), lambda b, h, i: (b, i, h)),
            pl.BlockSpec((None, seq, LANES), lambda b, h, i: (b, 0, h)),
            pl.BlockSpec((None, seq, LANES), lambda b, h, i: (b, 0, 0)),
            pl.BlockSpec((None, nk, V_HEAD_DIM, TK), lambda b, h, i: (b, 0, h, 0)),
        ],
        out_specs=pl.BlockSpec((None, TQ, V_HEAD_DIM), lambda b, h, i: (b, i, h)),
        out_shape=jax.ShapeDtypeStruct((batch, seq, hv), BF16),
        scratch_shapes=[
            pltpu.VMEM((seq, Q_PAD_DIM), BF16),
            pltpu.VMEM((nk, V_AUG_DIM, TK), BF16),
            pltpu.VMEM((Q_PAD_DIM, TQ), BF16),
            pltpu.VMEM((TK, TQ), F32),
            pltpu.VMEM((TK, TQ), F32),
            pltpu.VMEM((1, TQ), F32),
            pltpu.VMEM((1, TQ), F32),
            pltpu.VMEM((TK, TQ), BF16),
            pltpu.VMEM((TK, TQ), BF16),
            pltpu.VMEM((1, TQ), F32),
            pltpu.VMEM((1, TQ), F32),
            pltpu.VMEM((V_AUG_DIM, TQ), F32),
            pltpu.VMEM((1, TQ), F32),
        ],
        compiler_params=_params("arbitrary", "arbitrary", "arbitrary"),
        name="mla_attn",
    )(proj3, kn3, kr3, vt4)


def _lru_kernel(x_ref, y_ref, wc_ref, bc_ref, wrg_ref, brg_ref, wig_ref, big_ref, lam_ref,
                o_ref, xbuf, a_s, b_s, hc_ref):
    si = pl.program_id(2)
    ts, cb = TS_LRU, CB_LRU

    @pl.when(si == 0)
    def _():
        xbuf[0:8, :] = jnp.zeros((8, cb), F32)
        hc_ref[...] = jnp.zeros((8, cb), F32)

    xbuf[8:8 + ts, :] = x_ref[...].astype(F32)
    xc = bc_ref[...]
    for k in range(LRU_CONV_WIDTH):
        xc = xc + xbuf[pl.ds(8 - (LRU_CONV_WIDTH - 1) + k, ts), :] * wc_ref[k:k + 1, :]
    xbuf[0:8, :] = xbuf[ts:ts + 8, :]

    xcb = xc.astype(BF16)
    r = jax.nn.sigmoid(jnp.dot(xcb, wrg_ref[...], preferred_element_type=F32) + brg_ref[...])
    ig = jax.nn.sigmoid(jnp.dot(xcb, wig_ref[...], preferred_element_type=F32) + big_ref[...])
    z = -lam_ref[...]
    softplus = jnp.maximum(z, 0.0) + jnp.log1p(jnp.exp(-jnp.abs(z)))
    a = jnp.exp((-LRU_C * softplus) * r)
    u = jnp.sqrt(1.0 - a * a) * (ig * xc)

    row = lax.broadcasted_iota(jnp.int32, (ts, cb), 0) & 7
    for d in (1, 2, 4):
        keep = row >= d
        a_sh = jnp.where(keep, pltpu.roll(a, d, 0), 1.0)
        u_sh = jnp.where(keep, pltpu.roll(u, d, 0), 0.0)
        u = a * u_sh + u
        a = a * a_sh
    a_s[...] = a
    b_s[...] = u

    hc = hc_ref[...]
    for g in range(ts // 8):
        hg = a_s[8 * g:8 * g + 8, :] * hc + b_s[8 * g:8 * g + 8, :]
        b_s[8 * g:8 * g + 8, :] = hg
        hc = jnp.broadcast_to(hg[7:8, :], (8, cb))
    hc_ref[...] = hc

    o_ref[...] = (b_s[...] * _gelu_tanh(y_ref[...].astype(F32))).astype(BF16)


def _lru(proj3, w_conv, b_conv, w_rg2, b_rg, w_ig2, b_ig, lam, batch, seq):
    ts, cb = TS_LRU, CB_LRU
    ncb = LRU_WIDTH // cb
    x_off = Q_PAD_COLS // cb
    y_off = x_off + ncb
    return pl.pallas_call(
        _lru_kernel,
        grid=(batch, ncb, seq // ts),
        in_specs=[
            pl.BlockSpec((None, ts, cb), lambda b, c, s: (b, s, x_off + c)),
            pl.BlockSpec((None, ts, cb), lambda b, c, s: (b, s, y_off + c)),
            pl.BlockSpec((LRU_CONV_WIDTH, cb), lambda b, c, s: (0, c)),
            pl.BlockSpec((1, cb), lambda b, c, s: (0, c)),
            pl.BlockSpec((None, cb, cb), lambda b, c, s: (c, 0, 0)),
            pl.BlockSpec((1, cb), lambda b, c, s: (0, c)),
            pl.BlockSpec((None, cb, cb), lambda b, c, s: (c, 0, 0)),
            pl.BlockSpec((1, cb), lambda b, c, s: (0, c)),
            pl.BlockSpec((1, cb), lambda b, c, s: (0, c)),
        ],
        out_specs=pl.BlockSpec((None, ts, cb), lambda b, c, s: (b, s, c)),
        out_shape=jax.ShapeDtypeStruct((batch, seq, LRU_WIDTH), BF16),
        scratch_shapes=[
            pltpu.VMEM((ts + 8, cb), F32),
            pltpu.VMEM((ts, cb), F32),
            pltpu.VMEM((ts, cb), F32),
            pltpu.VMEM((8, cb), F32),
        ],
        compiler_params=_params("arbitrary", "arbitrary", "arbitrary"),
        name="lru",
    )(proj3, proj3, w_conv, b_conv, w_rg2, b_rg, w_ig2, b_ig, lam)


def _merge_kernel(attn_ref, lru_ref, gm_ref, gl_ref, x_ref, wom_ref, wol_ref, wout_ref, g_ref,
                  o_ref):
    o_mla = jnp.dot(attn_ref[...], wom_ref[...], preferred_element_type=F32)
    o_lru = jnp.dot(lru_ref[...], wol_ref[...], preferred_element_type=F32)
    merged = (jax.nn.sigmoid(gm_ref[...].astype(F32)) * o_mla
              + jax.nn.sigmoid(gl_ref[...].astype(F32)) * o_lru)
    y = jnp.dot(merged.astype(BF16), wout_ref[...], preferred_element_type=F32)
    o_ref[...] = x_ref[...] + _rms(y, g_ref[...])


def _merge(attn2, lru2, proj2, x2d, w_om, w_ol, w_out, g_post):
    t = x2d.shape[0]
    tm = TM_MERGE
    gm_off = (Q_PAD_COLS + 2 * LRU_WIDTH) // D_MODEL
    row = lambda i: (i, 0)
    const = lambda i: (0, 0)
    return pl.pallas_call(
        _merge_kernel,
        grid=(t // tm,),
        in_specs=[
            pl.BlockSpec((tm, D_MODEL), row),
            pl.BlockSpec((tm, D_MODEL), row),
            pl.BlockSpec((tm, D_MODEL), lambda i: (i, gm_off)),
            pl.BlockSpec((tm, D_MODEL), lambda i: (i, gm_off + 1)),
            pl.BlockSpec((tm, D_MODEL), row),
            _resident((D_MODEL, D_MODEL), const),
            _resident((D_MODEL, D_MODEL), const),
            _resident((D_MODEL, D_MODEL), const),
            pl.BlockSpec((1, D_MODEL), const),
        ],
        out_specs=pl.BlockSpec((tm, D_MODEL), row),
        out_shape=jax.ShapeDtypeStruct((t, D_MODEL), F32),
        compiler_params=_params("arbitrary"),
        name="merge",
    )(attn2, lru2, proj2, proj2, x2d, w_om, w_ol, w_out, g_post)


def _memkv_kernel(mem_ref, g_ref, wckt_ref, wcv_ref, ckt_ref, cv_ref):
    mn = _rms(mem_ref[...], g_ref[...]).astype(BF16)
    ckt_ref[...] = lax.dot_general(wckt_ref[...], mn, (((1,), (1,)), ((), ())),
                                   preferred_element_type=F32).astype(BF16)
    cv_ref[...] = jnp.dot(mn, wcv_ref[...], preferred_element_type=F32).astype(BF16)


def _mem_kv(mem, g_mem, w_ckt, w_cv):
    b, m, _ = mem.shape
    xd = X_HEADS * X_HEAD_DIM
    return pl.pallas_call(
        _memkv_kernel,
        grid=(b,),
        in_specs=[
            pl.BlockSpec((None, m, D_MODEL), lambda i: (i, 0, 0)),
            pl.BlockSpec((1, D_MODEL), lambda i: (0, 0)),
            pl.BlockSpec((xd, D_MODEL), lambda i: (0, 0)),
            pl.BlockSpec((D_MODEL, xd), lambda i: (0, 0)),
        ],
        out_specs=[
            pl.BlockSpec((None, xd, m), lambda i: (i, 0, 0)),
            pl.BlockSpec((None, m, xd), lambda i: (i, 0, 0)),
        ],
        out_shape=[
            jax.ShapeDtypeStruct((b, xd, m), BF16),
            jax.ShapeDtypeStruct((b, m, xd), BF16),
        ],
        compiler_params=_params("arbitrary"),
        name="mem_kv",
    )(mem, g_mem, w_ckt, w_cv)


def _cross_kernel(x_ref, gpre_ref, wcq_ref, ckt_ref, cv_ref, wco_ref, gpost_ref, o_ref):
    x = x_ref[...]
    h = _rms(x, gpre_ref[...]).astype(BF16)
    cq = jnp.dot(h, wcq_ref[...], preferred_element_type=F32).astype(BF16)
    outs = []
    for hh in range(X_HEADS):
        sl = slice(hh * X_HEAD_DIM, (hh + 1) * X_HEAD_DIM)
        s = jnp.dot(cq[:, sl], ckt_ref[sl, :], preferred_element_type=F32)
        p = jnp.exp2(s - jnp.max(s, axis=-1, keepdims=True))
        l = jnp.sum(p, axis=-1, keepdims=True)
        pv = jnp.dot(p.astype(BF16), cv_ref[:, sl], preferred_element_type=F32)
        outs.append((pv * (1.0 / l)).astype(BF16))
    co = jnp.concatenate(outs, axis=-1)
    y = jnp.dot(co, wco_ref[...], preferred_element_type=F32)
    o_ref[...] = x + _rms(y, gpost_ref[...])


def _cross(x2d, g_pre, w_cq, ckt, cv, w_co, g_post, seq):
    t = x2d.shape[0]
    tm = TM_CROSS
    xd = X_HEADS * X_HEAD_DIM
    m = cv.shape[1]
    per_b = seq // tm
    return pl.pallas_call(
        _cross_kernel,
        grid=(t // tm,),
        in_specs=[
            pl.BlockSpec((tm, D_MODEL), lambda i: (i, 0)),
            pl.BlockSpec((1, D_MODEL), lambda i: (0, 0)),
            pl.BlockSpec((D_MODEL, xd), lambda i: (0, 0)),
            pl.BlockSpec((None, xd, m), lambda i: (i // per_b, 0, 0)),
            pl.BlockSpec((None, m, xd), lambda i: (i // per_b, 0, 0)),
            pl.BlockSpec((xd, D_MODEL), lambda i: (0, 0)),
            pl.BlockSpec((1, D_MODEL), lambda i: (0, 0)),
        ],
        out_specs=pl.BlockSpec((tm, D_MODEL), lambda i: (i, 0)),
        out_shape=jax.ShapeDtypeStruct((t, D_MODEL), F32),
        compiler_params=_params("arbitrary"),
        name="cross",
    )(x2d, g_pre, w_cq, ckt, cv, w_co, g_post)


def _up_kernel(x_ref, halo_ref, g_ref, wg_ref, wv_ref, cwg_ref, cwv_ref, cbg_ref, cbv_ref,
               o_ref, h_ref, ag_ref, av_ref, *, tiles_per_seq):
    i = pl.program_id(0)
    j = pl.program_id(1)
    tm, halo = TM_UP, HALO_UP

    @pl.when(j == 0)
    def _():
        g = g_ref[...]
        keep = jnp.where(i % tiles_per_seq == 0, 0.0, 1.0)
        h_ref[0:halo, :] = (_rms(halo_ref[...], g) * keep).astype(BF16)
        h_ref[halo:halo + tm, :] = _rms(x_ref[...], g).astype(BF16)

    h = h_ref[...]
    ag_ref[...] = jnp.dot(h, wg_ref[...], preferred_element_type=F32)
    av_ref[...] = jnp.dot(h, wv_ref[...], preferred_element_type=F32)

    def conv(a_ref, w_ref, b_ref):
        acc = b_ref[...]
        for k in range(FFN_CONV_WIDTH):
            acc = acc + a_ref[pl.ds(halo - (FFN_CONV_WIDTH - 1) + k, tm), :] * w_ref[k:k + 1, :]
        return acc

    gate = conv(ag_ref, cwg_ref, cbg_ref)
    val = conv(av_ref, cwv_ref, cbv_ref)
    o_ref[...] = (_gelu_tanh(gate) * val).astype(BF16)


def _ffn_up(x2d, g_pre, w_up, w_fconv, b_fconv, seq):
    t = x2d.shape[0]
    tm, tn, halo = TM_UP, TN_UP, HALO_UP
    nj = D_FF // tn
    return pl.pallas_call(
        functools.partial(_up_kernel, tiles_per_seq=seq // tm),
        grid=(t // tm, nj),
        in_specs=[
            pl.BlockSpec((tm, D_MODEL), lambda i, j: (i, 0)),
            pl.BlockSpec((halo, D_MODEL), lambda i, j: (jnp.maximum(i * (tm // halo) - 1, 0), 0)),
            pl.BlockSpec((1, D_MODEL), lambda i, j: (0, 0)),
            pl.BlockSpec((D_MODEL, tn), lambda i, j: (0, j)),
            pl.BlockSpec((D_MODEL, tn), lambda i, j: (0, j + nj)),
            pl.BlockSpec((FFN_CONV_WIDTH, tn), lambda i, j: (0, j)),
            pl.BlockSpec((FFN_CONV_WIDTH, tn), lambda i, j: (0, j + nj)),
            pl.BlockSpec((1, tn), lambda i, j: (0, j)),
            pl.BlockSpec((1, tn), lambda i, j: (0, j + nj)),
        ],
        out_specs=pl.BlockSpec((tm, tn), lambda i, j: (i, j)),
        out_shape=jax.ShapeDtypeStruct((t, D_FF), BF16),
        scratch_shapes=[
            pltpu.VMEM((tm + halo, D_MODEL), BF16),
            pltpu.VMEM((tm + halo, tn), F32),
            pltpu.VMEM((tm + halo, tn), F32),
        ],
        compiler_params=_params("arbitrary", "arbitrary"),
        name="ffn_up",
    )(x2d, x2d, g_pre, w_up, w_up, w_fconv, w_fconv, b_fconv, b_fconv)


def _down_kernel(a_ref, x_ref, w_ref, g_ref, o_ref):
    y = jnp.dot(a_ref[...], w_ref[...], preferred_element_type=F32)
    o_ref[...] = x_ref[...] + _rms(y, g_ref[...])


def _ffn_down(act, x2d, w_down, g_post):
    t = x2d.shape[0]
    tm = TM_DOWN
    return pl.pallas_call(
        _down_kernel,
        grid=(t // tm,),
        in_specs=[
            pl.BlockSpec((tm, D_FF), lambda i: (i, 0)),
            pl.BlockSpec((tm, D_MODEL), lambda i: (i, 0)),
            _resident((D_FF, D_MODEL), lambda i: (0, 0)),
            pl.BlockSpec((1, D_MODEL), lambda i: (0, 0)),
        ],
        out_specs=pl.BlockSpec((tm, D_MODEL), lambda i: (i, 0)),
        out_shape=jax.ShapeDtypeStruct((t, D_MODEL), F32),
        compiler_params=_params("arbitrary"),
        name="ffn_down",
    )(act, x2d, w_down, g_post)


def _rope_table(positions):
    half = QK_ROPE_DIM // 2
    inv_freq = ROPE_THETA ** (-jnp.arange(0, QK_ROPE_DIM, 2, dtype=F32) / QK_ROPE_DIM)
    ang = positions.astype(F32).reshape(-1, 1) * inv_freq
    cos, sin = jnp.cos(ang), jnp.sin(ang)
    z = jnp.zeros_like(cos)
    assert half * 4 == LANES
    return jnp.concatenate([cos, cos, z, z, -sin, z, z, z, z, sin, z, z], axis=-1)


def _block_diag_pairs(w):
    nb, d, _ = w.shape
    w = w.reshape(nb // 2, 2, d, d)
    z = jnp.zeros_like(w[:, 0])
    top = jnp.concatenate([w[:, 0], z], axis=-1)
    bot = jnp.concatenate([z, w[:, 1]], axis=-1)
    return jnp.concatenate([top, bot], axis=-2)


def kernel(x, mem, positions, g_pre_mix, g_post_mix, w_in, g_ckv, w_ukv, w_o_mla, w_conv_lru, b_conv_lru, w_rg, b_rg, w_ig, b_ig, lru_lambda, w_o_lru, w_out, g_pre_x, g_post_x, g_mem, w_cq, w_ck, w_cv, w_co, g_pre_ffn, g_post_ffn, w_up, w_fconv, b_fconv, w_down):
    batch, seq, d = x.shape
    assert d == D_MODEL and g_pre_mix.shape[0] == 1
    t = batch * seq
    x2d = x.reshape(t, d)
    l = 0

    q_scale = (QK_HEAD_DIM ** -0.5) * LOG2E
    wi = w_in[l]
    wq = wi[:, :Q_COLS].reshape(d, MLA_HEADS, QK_HEAD_DIM) * q_scale
    wq = jnp.pad(wq, ((0, 0), (0, 0), (0, Q_PAD_DIM - QK_HEAD_DIM))).reshape(d, Q_PAD_COLS)
    kv_end = Q_COLS + KV_LORA_RANK + QK_ROPE_DIM
    w_a = jnp.concatenate([wq, wi[:, kv_end:]], axis=1).astype(BF16)
    w_b = jnp.pad(wi[:, Q_COLS:kv_end], ((0, 0), (0, LANES - QK_ROPE_DIM))).astype(BF16)
    wukv = w_ukv[l].reshape(KV_LORA_RANK, MLA_HEADS, QK_NOPE_DIM + V_HEAD_DIM)
    w_uk = wukv[:, :, :QK_NOPE_DIM].reshape(KV_LORA_RANK, -1).astype(BF16)
    w_uvt = wukv[:, :, QK_NOPE_DIM:].reshape(KV_LORA_RANK, -1).T.astype(BF16)
    row = lambda v: v.reshape(1, -1)
    trig = _rope_table(positions)

    proj = _proj_in(x2d, row(g_pre_mix[l]), w_a, trig)
    kn, kr, vt = _kv_path(x2d, row(g_pre_mix[l]), w_b, row(g_ckv[l]), w_uk, w_uvt, trig,
                          batch, seq)
    proj3 = proj.reshape(batch, seq, -1)
    attn = _mla_attn(proj3, kn.reshape(batch, seq, -1), kr.reshape(batch, seq, LANES),
                     vt.reshape(batch, seq // TK, MLA_HEADS * V_HEAD_DIM, TK), batch, seq)
    lru = _lru(proj3, w_conv_lru[l], row(b_conv_lru[l]),
               _block_diag_pairs(w_rg[l]).astype(BF16), row(b_rg[l]),
               _block_diag_pairs(w_ig[l]).astype(BF16), row(b_ig[l]),
               row(lru_lambda[l]), batch, seq)
    x1 = _merge(attn.reshape(t, -1), lru.reshape(t, -1), proj, x2d,
                w_o_mla[l].astype(BF16), w_o_lru[l].astype(BF16), w_out[l].astype(BF16),
                row(g_post_mix[l]))

    c_scale = (X_HEAD_DIM ** -0.5) * LOG2E
    ckt, cv = _mem_kv(mem, row(g_mem[l]), w_ck[l].T.astype(BF16), w_cv[l].astype(BF16))
    x2 = _cross(x1, row(g_pre_x[l]), (w_cq[l] * c_scale).astype(BF16), ckt, cv,
                w_co[l].astype(BF16), row(g_post_x[l]), seq)

    act = _ffn_up(x2, row(g_pre_ffn[l]), w_up[l].astype(BF16), w_fconv[l], row(b_fconv[l]), seq)
    x3 = _ffn_down(act, x2, w_down[l].astype(BF16), row(g_post_ffn[l]))
    return x3.reshape(batch, seq, d)
```

```python
import functools
import math

import jax
import jax.numpy as jnp
from jax import lax
from jax.experimental import pallas as pl
from jax.experimental.pallas import tpu as pltpu

F32 = jnp.float32
BF16 = jnp.bfloat16

D_MODEL = 2048
MLA_HEADS = 16
QK_NOPE_DIM = 128
QK_ROPE_DIM = 64
V_HEAD_DIM = 128
KV_LORA_RANK = 512
QK_HEAD_DIM = QK_NOPE_DIM + QK_ROPE_DIM
ROPE_THETA = 10000.0
LRU_WIDTH = D_MODEL
LRU_BLOCKS = 16
LRU_BLOCK_DIM = LRU_WIDTH // LRU_BLOCKS
LRU_CONV_WIDTH = 4
LRU_C = 8.0
X_HEADS = 4
X_HEAD_DIM = 128
D_FF = 5632
FFN_CONV_WIDTH = 3
EPS = 1e-6

LOG2E = math.log2(math.e)
LANES = 128
BF16_ROWS = 16
Q_PAD_DIM = 2 * LANES
V_AUG_DIM = V_HEAD_DIM + BF16_ROWS
Q_COLS = MLA_HEADS * QK_HEAD_DIM
Q_PAD_COLS = MLA_HEADS * Q_PAD_DIM
VMEM_LIMIT = 56 * 1024 * 1024
NT_DIMS = (((1,), (1,)), ((), ()))

TM_PROJ, TN_PROJ = 1024, 1024
TM_KV = 512
TQ, TK = 1024, 1024
assert TQ == TK and TK % TM_KV == 0
Q_CHUNKS = 2
TS_LRU, CB_LRU = 512, 256
TM_MERGE = 256
TM_CROSS = 512
TM_UP, TN_UP, HALO_UP = 1024, 512, 16
TM_DOWN = 256


def _params(*sem):
    return pltpu.CompilerParams(dimension_semantics=sem, vmem_limit_bytes=VMEM_LIMIT)


def _resident(shape, index_map):
    return pl.BlockSpec(shape, index_map, pipeline_mode=pl.Buffered(1))


def _rms(x, g):
    return x * lax.rsqrt(jnp.mean(x * x, axis=-1, keepdims=True) + EPS) * g


def _gelu_tanh(x):
    c = math.sqrt(2.0 / math.pi)
    return 0.5 * x * (1.0 + jnp.tanh(c * (x + 0.044715 * (x * x * x))))


def _rope_slab(slab, trig_ref):
    c = trig_ref[:, 0:LANES]
    s1 = trig_ref[:, LANES:2 * LANES]
    s2 = trig_ref[:, 2 * LANES:3 * LANES]
    return slab * c + pltpu.roll(slab, 96, 1) * s1 + pltpu.roll(slab, 32, 1) * s2


def _proj_kernel(x_ref, g_ref, wq_ref, wr_ref, trig_ref, o_ref, h_ref, *, n_q_tiles):
    j = pl.program_id(1)

    @pl.when(j == 0)
    def _():
        h_ref[...] = _rms(x_ref[...], g_ref[...]).astype(BF16)

    @pl.when(j >= n_q_tiles)
    def _():
        o_ref[...] = lax.dot_general(h_ref[...], wr_ref[...], NT_DIMS,
                                     preferred_element_type=F32).astype(BF16)

    @pl.when(j < n_q_tiles)
    def _():
        acc = lax.dot_general(h_ref[...], wq_ref[...], NT_DIMS, preferred_element_type=F32)
        for hh in range(TN_PROJ // Q_PAD_DIM):
            c0 = hh * Q_PAD_DIM
            o_ref[:, c0:c0 + LANES] = acc[:, c0:c0 + LANES].astype(BF16)
            slab = acc[:, c0 + LANES:c0 + Q_PAD_DIM]
            o_ref[:, c0 + LANES:c0 + Q_PAD_DIM] = _rope_slab(slab, trig_ref).astype(BF16)


def _proj_in(x2d, g, wq_t, wr_t, trig):
    t = x2d.shape[0]
    nq = wq_t.shape[0] // TN_PROJ
    n = wq_t.shape[0] + wr_t.shape[0]
    return pl.pallas_call(
        functools.partial(_proj_kernel, n_q_tiles=nq),
        grid=(t // TM_PROJ, n // TN_PROJ),
        in_specs=[
            pl.BlockSpec((TM_PROJ, D_MODEL), lambda i, j: (i, 0)),
            pl.BlockSpec((1, D_MODEL), lambda i, j: (0, 0)),
            pl.BlockSpec((TN_PROJ, D_MODEL), lambda i, j: (jnp.minimum(j, nq - 1), 0)),
            pl.BlockSpec((TN_PROJ, D_MODEL), lambda i, j: (jnp.maximum(j - nq, 0), 0)),
            pl.BlockSpec((TM_PROJ, 3 * LANES), lambda i, j: (i, 0)),
        ],
        out_specs=pl.BlockSpec((TM_PROJ, TN_PROJ), lambda i, j: (i, j)),
        out_shape=jax.ShapeDtypeStruct((t, n), BF16),
        scratch_shapes=[pltpu.VMEM((TM_PROJ, D_MODEL), BF16)],
        compiler_params=_params("arbitrary", "arbitrary"),
        name="proj_in",
    )(x2d, g, wq_t, wr_t, trig)


def _kv_kernel(x_ref, g_ref, wb_ref, gc_ref, wuk_ref, wuvt_ref, trig_ref,
               kn_ref, kr_ref, vt_ref):
    h = _rms(x_ref[...], g_ref[...]).astype(BF16)
    c = lax.dot_general(h, wb_ref[...], NT_DIMS, preferred_element_type=F32)
    n = _rms(c[:, :KV_LORA_RANK], gc_ref[...]).astype(BF16)
    kn_ref[...] = jnp.dot(n, wuk_ref[...], preferred_element_type=F32).astype(BF16)
    vt_ref[...] = lax.dot_general(wuvt_ref[...], n, NT_DIMS,
                                  preferred_element_type=F32).astype(BF16)
    kr_ref[...] = _rope_slab(c[:, KV_LORA_RANK:], trig_ref).astype(BF16)


def _kv_path(x2d, g, w_b, g_ckv, w_uk, w_uvt, trig, batch, seq):
    t = x2d.shape[0]
    nk = seq // TM_KV
    hv = MLA_HEADS * V_HEAD_DIM
    return pl.pallas_call(
        _kv_kernel,
        grid=(t // TM_KV,),
        in_specs=[
            pl.BlockSpec((TM_KV, D_MODEL), lambda i: (i, 0)),
            pl.BlockSpec((1, D_MODEL), lambda i: (0, 0)),
            _resident(w_b.shape, lambda i: (0, 0)),
            pl.BlockSpec((1, KV_LORA_RANK), lambda i: (0, 0)),
            _resident(w_uk.shape, lambda i: (0, 0)),
            _resident(w_uvt.shape, lambda i: (0, 0)),
            pl.BlockSpec((TM_KV, 3 * LANES), lambda i: (i, 0)),
        ],
        out_specs=[
            pl.BlockSpec((TM_KV, hv), lambda i: (i, 0)),
            pl.BlockSpec((TM_KV, LANES), lambda i: (i, 0)),
            pl.BlockSpec((None, hv, TM_KV), lambda i: (i, 0, 0)),
        ],
        out_shape=[
            jax.ShapeDtypeStruct((t, hv), BF16),
            jax.ShapeDtypeStruct((t, LANES), BF16),
            jax.ShapeDtypeStruct((batch * nk, hv, TM_KV), BF16),
        ],
        compiler_params=_params("arbitrary"),
        name="kv_path",
    )(x2d, g, w_b, g_ckv, w_uk, w_uvt, trig)


def _attn_kernel(q_ref, kn_ref, kr_ref, vt_ref, o_ref,
                 kcat_ref, vaug_ref, qt_ref, s0, s1, x0, x1, p0, p1, a0, a1, acc_ref, m_ref):
    i = pl.program_id(2)
    n = i + 1

    @pl.when(i == 0)
    def _():
        kcat_ref[:, 0:LANES] = kn_ref[...]
        kcat_ref[:, LANES:Q_PAD_DIM] = kr_ref[...]
        sub = TK // TM_KV
        for j in range(vaug_ref.shape[0]):
            for c in range(sub):
                vaug_ref[j, 0:V_HEAD_DIM, c * TM_KV:(c + 1) * TM_KV] = vt_ref[j * sub + c]
        vaug_ref[:, V_HEAD_DIM:V_AUG_DIM, :] = jnp.ones(
            (vaug_ref.shape[0], V_AUG_DIM - V_HEAD_DIM, TK), BF16)

    qt_ref[...] = q_ref[...].astype(F32).T.astype(BF16)
    m_ref[...] = jnp.full(m_ref.shape, -jnp.inf, F32)
    acc_ref[...] = jnp.zeros(acc_ref.shape, F32)

    def qk(b, s_ref, x_ref, cols):
        k0 = pl.multiple_of(b * TK, TK)
        s = jnp.dot(kcat_ref[pl.ds(k0, TK), :], qt_ref[:, cols],
                    preferred_element_type=F32)
        s_ref[:, cols] = s
        x_ref[:, cols] = jnp.max(s, axis=0, keepdims=True)

    def sm(s_ref, x_ref, p_ref, a_ref, diagonal, cols):
        s = s_ref[:, cols]
        m_old = m_ref[:, cols]
        if diagonal:
            key = lax.broadcasted_iota(jnp.int32, s.shape, 0)
            qry = lax.broadcasted_iota(jnp.int32, s.shape, 1) + cols.start
            s = jnp.where(key <= qry, s, -jnp.inf)
            m_new = jnp.maximum(m_old, jnp.max(s, axis=0, keepdims=True))
        else:
            m_new = jnp.maximum(m_old, x_ref[:, cols])
        p = jnp.exp2(s - m_new)
        alpha = jnp.exp2(m_old - m_new)
        m_ref[:, cols] = m_new
        a_ref[:, cols] = alpha
        p_ref[:, cols] = p.astype(BF16)

    def pv(b, p_ref, a_ref, cols):
        acc_ref[:, cols] = acc_ref[:, cols] * a_ref[:, cols] + jnp.dot(
            vaug_ref[b], p_ref[:, cols], preferred_element_type=F32)

    bufs = ((s0, x0, p0, a0), (s1, x1, p1, a1))

    def step(t, parity, do_qk=False, do_sm=False, diagonal=False, do_pv=False):
        s_c, x_c, p_c, a_c = bufs[parity]
        s_p, x_p, p_p, a_p = bufs[1 - parity]
        for c in range(Q_CHUNKS):
            cols = slice(c * (TQ // Q_CHUNKS), (c + 1) * (TQ // Q_CHUNKS))
            if do_qk:
                qk(t, s_c, x_c, cols)
            if do_sm:
                sm(s_p, x_p, p_p, a_p, diagonal, cols)
            if do_pv:
                pv(t - 2, p_c, a_c, cols)

    def by_parity(t, cond, **stages):
        for parity in (0, 1):
            @pl.when(jnp.logical_and(cond, t % 2 == parity))
            def _():
                step(t, parity, **stages)

    step(0, 0, do_qk=True)

    @pl.when(n >= 2)
    def _():
        step(1, 1, do_qk=True, do_sm=True)

    @pl.when(n == 1)
    def _():
        step(1, 1, do_sm=True, diagonal=True)

    def steady(t, carry):
        by_parity(t, True, do_qk=True, do_sm=True, do_pv=True)
        return carry

    lax.fori_loop(2, n, steady, 0)
    by_parity(n, n >= 2, do_sm=True, diagonal=True, do_pv=True)
    by_parity(n + 1, True, do_pv=True)

    out = acc_ref[0:V_HEAD_DIM, :] * (1.0 / acc_ref[V_HEAD_DIM:V_HEAD_DIM + 1, :])
    o_ref[...] = out.T.astype(BF16)


def _mla_attn(proj3, kn3, kr3, vt4, batch, seq):
    nk = seq // TK
    hv = MLA_HEADS * V_HEAD_DIM
    return pl.pallas_call(
        _attn_kernel,
        grid=(batch, MLA_HEADS, seq // TQ),
        in_specs=[
            pl.BlockSpec((None, TQ, Q_PAD_DIM), lambda b, h, i: (b, i, h)),
            pl.BlockSpec((None, seq, LANES), lambda b, h, i: (b, 0, h)),
            pl.BlockSpec((None, seq, LANES), lambda b, h, i: (b, 0, 0)),
            pl.BlockSpec((None, seq // TM_KV, V_HEAD_DIM, TM_KV), lambda b, h, i: (b, 0, h, 0)),
        ],
        out_specs=pl.BlockSpec((None, TQ, V_HEAD_DIM), lambda b, h, i: (b, i, h)),
        out_shape=jax.ShapeDtypeStruct((batch, seq, hv), BF16),
        scratch_shapes=[
            pltpu.VMEM((seq, Q_PAD_DIM), BF16),
            pltpu.VMEM((nk, V_AUG_DIM, TK), BF16),
            pltpu.VMEM((Q_PAD_DIM, TQ), BF16),
            pltpu.VMEM((TK, TQ), F32),
            pltpu.VMEM((TK, TQ), F32),
            pltpu.VMEM((1, TQ), F32),
            pltpu.VMEM((1, TQ), F32),
            pltpu.VMEM((TK, TQ), BF16),
            pltpu.VMEM((TK, TQ), BF16),
            pltpu.VMEM((1, TQ), F32),
            pltpu.VMEM((1, TQ), F32),
            pltpu.VMEM((V_AUG_DIM, TQ), F32),
            pltpu.VMEM((1, TQ), F32),
        ],
        compiler_params=_params("arbitrary", "arbitrary", "arbitrary"),
        name="mla_attn",
    )(proj3, kn3, kr3, vt4)


def _lru_kernel(x_ref, y_ref, wc_ref, bc_ref, wrg_ref, brg_ref, wig_ref, big_ref, lam_ref,
                o_ref, xprev, a_s, b_s, hc_ref):
    si = pl.program_id(2)
    ts, cb = TS_LRU, CB_LRU

    @pl.when(si == 0)
    def _():
        xprev[...] = jnp.zeros((8, cb), F32)
        hc_ref[...] = jnp.zeros((8, cb), F32)

    x = x_ref[...].astype(F32)
    prev = xprev[...]
    row8 = lax.broadcasted_iota(jnp.int32, (8, cb), 0)
    last = LRU_CONV_WIDTH - 1
    xc = bc_ref[...] + x * wc_ref[last:last + 1, :]
    for k in range(1, LRU_CONV_WIDTH):
        xs = pltpu.roll(x, k, 0)
        head = jnp.where(row8 < k, pltpu.roll(prev, k, 0), xs[0:8, :])
        xs = jnp.concatenate([head, xs[8:, :]], axis=0)
        xc = xc + xs * wc_ref[last - k:last - k + 1, :]
    xprev[...] = x[ts - 8:, :]

    xcb = xc.astype(BF16)
    r = jax.nn.sigmoid(jnp.dot(xcb, wrg_ref[...], preferred_element_type=F32) + brg_ref[...])
    ig = jax.nn.sigmoid(jnp.dot(xcb, wig_ref[...], preferred_element_type=F32) + big_ref[...])
    z = -lam_ref[...]
    softplus = jnp.maximum(z, 0.0) + jnp.log1p(jnp.exp(-jnp.abs(z)))
    a = jnp.exp((-LRU_C * softplus) * r)
    v = 1.0 - a * a
    u = jnp.where(v > 0.0, v * lax.rsqrt(v), 0.0) * (ig * xc)

    ng = ts // 8
    a = a.reshape(ng, 8, cb)
    u = u.reshape(ng, 8, cb)
    row = lax.broadcasted_iota(jnp.int32, (ng, 8, cb), 1)
    for d in (1, 2, 4):
        keep = row >= d
        a_sh = pltpu.roll(a, d, 1)
        u_sh = pltpu.roll(u, d, 1)
        u = jnp.where(keep, a, 0.0) * u_sh + u
        a = a * jnp.where(keep, a_sh, 1.0)
    a_s[...] = a.reshape(ts, cb)
    b_s[...] = u.reshape(ts, cb)

    hc = hc_ref[...]
    for g in range(ts // 8):
        hg = a_s[8 * g:8 * g + 8, :] * hc + b_s[8 * g:8 * g + 8, :]
        b_s[8 * g:8 * g + 8, :] = hg
        hc = jnp.broadcast_to(hg[7:8, :], (8, cb))
    hc_ref[...] = hc

    o_ref[...] = (b_s[...] * _gelu_tanh(y_ref[...].astype(F32))).astype(BF16)


def _lru(proj3, w_conv, b_conv, w_rg2, b_rg, w_ig2, b_ig, lam, batch, seq):
    ts, cb = TS_LRU, CB_LRU
    ncb = LRU_WIDTH // cb
    x_off = Q_PAD_COLS // cb
    y_off = x_off + ncb
    return pl.pallas_call(
        _lru_kernel,
        grid=(batch, ncb, seq // ts),
        in_specs=[
            pl.BlockSpec((None, ts, cb), lambda b, c, s: (b, s, x_off + c)),
            pl.BlockSpec((None, ts, cb), lambda b, c, s: (b, s, y_off + c)),
            pl.BlockSpec((LRU_CONV_WIDTH, cb), lambda b, c, s: (0, c)),
            pl.BlockSpec((1, cb), lambda b, c, s: (0, c)),
            pl.BlockSpec((None, cb, cb), lambda b, c, s: (c, 0, 0)),
            pl.BlockSpec((1, cb), lambda b, c, s: (0, c)),
            pl.BlockSpec((None, cb, cb), lambda b, c, s: (c, 0, 0)),
            pl.BlockSpec((1, cb), lambda b, c, s: (0, c)),
            pl.BlockSpec((1, cb), lambda b, c, s: (0, c)),
        ],
        out_specs=pl.BlockSpec((None, ts, cb), lambda b, c, s: (b, s, c)),
        out_shape=jax.ShapeDtypeStruct((batch, seq, LRU_WIDTH), BF16),
        scratch_shapes=[
            pltpu.VMEM((8, cb), F32),
            pltpu.VMEM((ts, cb), F32),
            pltpu.VMEM((ts, cb), F32),
            pltpu.VMEM((8, cb), F32),
        ],
        compiler_params=_params("arbitrary", "arbitrary", "arbitrary"),
        name="lru",
    )(proj3, proj3, w_conv, b_conv, w_rg2, b_rg, w_ig2, b_ig, lam)


def _merge_kernel(attn_ref, lru_ref, gm_ref, gl_ref, x_ref, wom_ref, wol_ref, wout_ref, g_ref,
                  o_ref):
    o_mla = jnp.dot(attn_ref[...], wom_ref[...], preferred_element_type=F32)
    o_lru = jnp.dot(lru_ref[...], wol_ref[...], preferred_element_type=F32)
    merged = (jax.nn.sigmoid(gm_ref[...].astype(F32)) * o_mla
              + jax.nn.sigmoid(gl_ref[...].astype(F32)) * o_lru)
    y = jnp.dot(merged.astype(BF16), wout_ref[...], preferred_element_type=F32)
    o_ref[...] = x_ref[...] + _rms(y, g_ref[...])


def _merge(attn2, lru2, proj2, x2d, w_om, w_ol, w_out, g_post):
    t = x2d.shape[0]
    tm = TM_MERGE
    gm_off = (Q_PAD_COLS + 2 * LRU_WIDTH) // D_MODEL
    row = lambda i: (i, 0)
    const = lambda i: (0, 0)
    return pl.pallas_call(
        _merge_kernel,
        grid=(t // tm,),
        in_specs=[
            pl.BlockSpec((tm, D_MODEL), row),
            pl.BlockSpec((tm, D_MODEL), row),
            pl.BlockSpec((tm, D_MODEL), lambda i: (i, gm_off)),
            pl.BlockSpec((tm, D_MODEL), lambda i: (i, gm_off + 1)),
            pl.BlockSpec((tm, D_MODEL), row),
            _resident((D_MODEL, D_MODEL), const),
            _resident((D_MODEL, D_MODEL), const),
            _resident((D_MODEL, D_MODEL), const),
            pl.BlockSpec((1, D_MODEL), const),
        ],
        out_specs=pl.BlockSpec((tm, D_MODEL), row),
        out_shape=jax.ShapeDtypeStruct((t, D_MODEL), F32),
        compiler_params=_params("arbitrary"),
        name="merge",
    )(attn2, lru2, proj2, proj2, x2d, w_om, w_ol, w_out, g_post)


def _memkv_kernel(mem_ref, g_ref, wckt_ref, wcv_ref, ckt_ref, cv_ref):
    mn = _rms(mem_ref[...], g_ref[...]).astype(BF16)
    ckt_ref[...] = lax.dot_general(wckt_ref[...], mn, (((1,), (1,)), ((), ())),
                                   preferred_element_type=F32).astype(BF16)
    cv_ref[...] = jnp.dot(mn, wcv_ref[...], preferred_element_type=F32).astype(BF16)


def _mem_kv(mem, g_mem, w_ckt, w_cv):
    b, m, _ = mem.shape
    xd = X_HEADS * X_HEAD_DIM
    return pl.pallas_call(
        _memkv_kernel,
        grid=(b,),
        in_specs=[
            pl.BlockSpec((None, m, D_MODEL), lambda i: (i, 0, 0)),
            pl.BlockSpec((1, D_MODEL), lambda i: (0, 0)),
            pl.BlockSpec((xd, D_MODEL), lambda i: (0, 0)),
            pl.BlockSpec((D_MODEL, xd), lambda i: (0, 0)),
        ],
        out_specs=[
            pl.BlockSpec((None, xd, m), lambda i: (i, 0, 0)),
            pl.BlockSpec((None, m, xd), lambda i: (i, 0, 0)),
        ],
        out_shape=[
            jax.ShapeDtypeStruct((b, xd, m), BF16),
            jax.ShapeDtypeStruct((b, m, xd), BF16),
        ],
        compiler_params=_params("arbitrary"),
        name="mem_kv",
    )(mem, g_mem, w_ckt, w_cv)


def _cross_kernel(x_ref, gpre_ref, wcq_ref, ckt_ref, cv_ref, wco_ref, gpost_ref, o_ref):
    x = x_ref[...]
    h = _rms(x, gpre_ref[...]).astype(BF16)
    cq = jnp.dot(h, wcq_ref[...], preferred_element_type=F32).astype(BF16)
    outs = []
    for hh in range(X_HEADS):
        sl = slice(hh * X_HEAD_DIM, (hh + 1) * X_HEAD_DIM)
        s = jnp.dot(cq[:, sl], ckt_ref[sl, :], preferred_element_type=F32)
        p = jnp.exp2(s - jnp.max(s, axis=-1, keepdims=True))
        l = jnp.sum(p, axis=-1, keepdims=True)
        pv = jnp.dot(p.astype(BF16), cv_ref[:, sl], preferred_element_type=F32)
        outs.append((pv * (1.0 / l)).astype(BF16))
    co = jnp.concatenate(outs, axis=-1)
    y = jnp.dot(co, wco_ref[...], preferred_element_type=F32)
    o_ref[...] = x + _rms(y, gpost_ref[...])


def _cross(x2d, g_pre, w_cq, ckt, cv, w_co, g_post, seq):
    t = x2d.shape[0]
    tm = TM_CROSS
    xd = X_HEADS * X_HEAD_DIM
    m = cv.shape[1]
    per_b = seq // tm
    return pl.pallas_call(
        _cross_kernel,
        grid=(t // tm,),
        in_specs=[
            pl.BlockSpec((tm, D_MODEL), lambda i: (i, 0)),
            pl.BlockSpec((1, D_MODEL), lambda i: (0, 0)),
            pl.BlockSpec((D_MODEL, xd), lambda i: (0, 0)),
            pl.BlockSpec((None, xd, m), lambda i: (i // per_b, 0, 0)),
            pl.BlockSpec((None, m, xd), lambda i: (i // per_b, 0, 0)),
            pl.BlockSpec((xd, D_MODEL), lambda i: (0, 0)),
            pl.BlockSpec((1, D_MODEL), lambda i: (0, 0)),
        ],
        out_specs=pl.BlockSpec((tm, D_MODEL), lambda i: (i, 0)),
        out_shape=jax.ShapeDtypeStruct((t, D_MODEL), F32),
        compiler_params=_params("arbitrary"),
        name="cross",
    )(x2d, g_pre, w_cq, ckt, cv, w_co, g_post)


def _up_kernel(x_ref, halo_ref, g_ref, wg_ref, wv_ref, cwg_ref, cwv_ref, cbg_ref, cbv_ref,
               o_ref, h_ref, ag_ref, av_ref, *, tiles_per_seq):
    i = pl.program_id(0)
    j = pl.program_id(1)
    tm, halo = TM_UP, HALO_UP

    @pl.when(j == 0)
    def _():
        g = g_ref[...]
        keep = jnp.where(i % tiles_per_seq == 0, 0.0, 1.0)
        h_ref[0:halo, :] = (_rms(halo_ref[...], g) * keep).astype(BF16)
        h_ref[halo:halo + tm, :] = _rms(x_ref[...], g).astype(BF16)

    h = h_ref[...]
    ag_ref[...] = jnp.dot(h, wg_ref[...], preferred_element_type=F32)
    av_ref[...] = jnp.dot(h, wv_ref[...], preferred_element_type=F32)

    def conv(a_ref, w_ref, b_ref):
        acc = b_ref[...]
        for k in range(FFN_CONV_WIDTH):
            acc = acc + a_ref[pl.ds(halo - (FFN_CONV_WIDTH - 1) + k, tm), :] * w_ref[k:k + 1, :]
        return acc

    gate = conv(ag_ref, cwg_ref, cbg_ref)
    val = conv(av_ref, cwv_ref, cbv_ref)
    o_ref[...] = (_gelu_tanh(gate) * val).astype(BF16)


def _ffn_up(x2d, g_pre, w_up, w_fconv, b_fconv, seq):
    t = x2d.shape[0]
    tm, tn, halo = TM_UP, TN_UP, HALO_UP
    nj = D_FF // tn
    return pl.pallas_call(
        functools.partial(_up_kernel, tiles_per_seq=seq // tm),
        grid=(t // tm, nj),
        in_specs=[
            pl.BlockSpec((tm, D_MODEL), lambda i, j: (i, 0)),
            pl.BlockSpec((halo, D_MODEL), lambda i, j: (jnp.maximum(i * (tm // halo) - 1, 0), 0)),
            pl.BlockSpec((1, D_MODEL), lambda i, j: (0, 0)),
            pl.BlockSpec((D_MODEL, tn), lambda i, j: (0, j)),
            pl.BlockSpec((D_MODEL, tn), lambda i, j: (0, j + nj)),
            pl.BlockSpec((FFN_CONV_WIDTH, tn), lambda i, j: (0, j)),
            pl.BlockSpec((FFN_CONV_WIDTH, tn), lambda i, j: (0, j + nj)),
            pl.BlockSpec((1, tn), lambda i, j: (0, j)),
            pl.BlockSpec((1, tn), lambda i, j: (0, j + nj)),
        ],
        out_specs=pl.BlockSpec((tm, tn), lambda i, j: (i, j)),
        out_shape=jax.ShapeDtypeStruct((t, D_FF), BF16),
        scratch_shapes=[
            pltpu.VMEM((tm + halo, D_MODEL), BF16),
            pltpu.VMEM((tm + halo, tn), F32),
            pltpu.VMEM((tm + halo, tn), F32),
        ],
        compiler_params=_params("arbitrary", "arbitrary"),
        name="ffn_up",
    )(x2d, x2d, g_pre, w_up, w_up, w_fconv, w_fconv, b_fconv, b_fconv)


def _down_kernel(a_ref, x_ref, w_ref, g_ref, o_ref):
    y = jnp.dot(a_ref[...], w_ref[...], preferred_element_type=F32)
    o_ref[...] = x_ref[...] + _rms(y, g_ref[...])


def _ffn_down(act, x2d, w_down, g_post):
    t = x2d.shape[0]
    tm = TM_DOWN
    return pl.pallas_call(
        _down_kernel,
        grid=(t // tm,),
        in_specs=[
            pl.BlockSpec((tm, D_FF), lambda i: (i, 0)),
            pl.BlockSpec((tm, D_MODEL), lambda i: (i, 0)),
            _resident((D_FF, D_MODEL), lambda i: (0, 0)),
            pl.BlockSpec((1, D_MODEL), lambda i: (0, 0)),
        ],
        out_specs=pl.BlockSpec((tm, D_MODEL), lambda i: (i, 0)),
        out_shape=jax.ShapeDtypeStruct((t, D_MODEL), F32),
        compiler_params=_params("arbitrary"),
        name="ffn_down",
    )(act, x2d, w_down, g_post)


def _rope_table(positions):
    half = QK_ROPE_DIM // 2
    inv_freq = ROPE_THETA ** (-jnp.arange(0, QK_ROPE_DIM, 2, dtype=F32) / QK_ROPE_DIM)
    ang = positions.astype(F32).reshape(-1, 1) * inv_freq
    cos, sin = jnp.cos(ang), jnp.sin(ang)
    z = jnp.zeros_like(cos)
    assert half * 4 == LANES
    return jnp.concatenate([cos, cos, z, z, -sin, z, z, z, z, sin, z, z], axis=-1)


def _block_diag_pairs(w):
    nb, d, _ = w.shape
    w = w.reshape(nb // 2, 2, d, d)
    z = jnp.zeros_like(w[:, 0])
    top = jnp.concatenate([w[:, 0], z], axis=-1)
    bot = jnp.concatenate([z, w[:, 1]], axis=-1)
    return jnp.concatenate([top, bot], axis=-2)


def kernel(x, mem, positions, g_pre_mix, g_post_mix, w_in, g_ckv, w_ukv, w_o_mla, w_conv_lru, b_conv_lru, w_rg, b_rg, w_ig, b_ig, lru_lambda, w_o_lru, w_out, g_pre_x, g_post_x, g_mem, w_cq, w_ck, w_cv, w_co, g_pre_ffn, g_post_ffn, w_up, w_fconv, b_fconv, w_down):
    batch, seq, d = x.shape
    assert d == D_MODEL and g_pre_mix.shape[0] == 1
    t = batch * seq
    x2d = x.reshape(t, d)
    l = 0

    q_scale = (QK_HEAD_DIM ** -0.5) * LOG2E
    wi_t = w_in[l].T
    wq_t = wi_t[:Q_COLS].reshape(MLA_HEADS, QK_HEAD_DIM, d) * q_scale
    wq_t = jnp.pad(wq_t, ((0, 0), (0, Q_PAD_DIM - QK_HEAD_DIM), (0, 0)))
    wq_t = wq_t.reshape(Q_PAD_COLS, d).astype(BF16)
    kv_end = Q_COLS + KV_LORA_RANK + QK_ROPE_DIM
    wr_t = wi_t[kv_end:].astype(BF16)
    wb_t = jnp.pad(wi_t[Q_COLS:kv_end], ((0, LANES - QK_ROPE_DIM), (0, 0))).astype(BF16)
    wukv = w_ukv[l].reshape(KV_LORA_RANK, MLA_HEADS, QK_NOPE_DIM + V_HEAD_DIM)
    w_uk = wukv[:, :, :QK_NOPE_DIM].reshape(KV_LORA_RANK, -1).astype(BF16)
    w_uvt = wukv[:, :, QK_NOPE_DIM:].reshape(KV_LORA_RANK, -1).T.astype(BF16)
    row = lambda v: v.reshape(1, -1)
    trig = _rope_table(positions)

    proj = _proj_in(x2d, row(g_pre_mix[l]), wq_t, wr_t, trig)
    kn, kr, vt = _kv_path(x2d, row(g_pre_mix[l]), wb_t, row(g_ckv[l]), w_uk, w_uvt, trig,
                          batch, seq)
    proj3 = proj.reshape(batch, seq, -1)
    attn = _mla_attn(proj3, kn.reshape(batch, seq, -1), kr.reshape(batch, seq, LANES),
                     vt.reshape(batch, seq // TM_KV, MLA_HEADS * V_HEAD_DIM, TM_KV), batch, seq)
    lru = _lru(proj3, w_conv_lru[l], row(b_conv_lru[l]),
               _block_diag_pairs(w_rg[l]).astype(BF16), row(b_rg[l]),
               _block_diag_pairs(w_ig[l]).astype(BF16), row(b_ig[l]),
               row(lru_lambda[l]), batch, seq)
    x1 = _merge(attn.reshape(t, -1), lru.reshape(t, -1), proj, x2d,
                w_o_mla[l].astype(BF16), w_o_lru[l].astype(BF16), w_out[l].astype(BF16),
                row(g_post_mix[l]))

    c_scale = (X_HEAD_DIM ** -0.5) * LOG2E
    ckt, cv = _mem_kv(mem, row(g_mem[l]), w_ck[l].T.astype(BF16), w_cv[l].astype(BF16))
    x2 = _cross(x1, row(g_pre_x[l]), (w_cq[l] * c_scale).astype(BF16), ckt, cv,
                w_co[l].astype(BF16), row(g_post_x[l]), seq)

    act = _ffn_up(x2, row(g_pre_ffn[l]), w_up[l].astype(BF16), w_fconv[l], row(b_fconv[l]), seq)
    x3 = _ffn_down(act, x2, w_down[l].astype(BF16), row(g_post_ffn[l]))
    return x3.reshape(batch, seq, d)
```

```python
import functools
import math

import jax
import jax.numpy as jnp
from jax import lax
from jax.experimental import pallas as pl
from jax.experimental.pallas import tpu as pltpu

F32 = jnp.float32
BF16 = jnp.bfloat16

D_MODEL = 2048
MLA_HEADS = 16
QK_NOPE_DIM = 128
QK_ROPE_DIM = 64
V_HEAD_DIM = 128
KV_LORA_RANK = 512
QK_HEAD_DIM = QK_NOPE_DIM + QK_ROPE_DIM
ROPE_THETA = 10000.0
LRU_WIDTH = D_MODEL
LRU_BLOCKS = 16
LRU_BLOCK_DIM = LRU_WIDTH // LRU_BLOCKS
LRU_CONV_WIDTH = 4
LRU_C = 8.0
X_HEADS = 4
X_HEAD_DIM = 128
D_FF = 5632
FFN_CONV_WIDTH = 3
EPS = 1e-6

LOG2E = math.log2(math.e)
LANES = 128
BF16_ROWS = 16
Q_PAD_DIM = 2 * LANES
V_AUG_DIM = V_HEAD_DIM + BF16_ROWS
Q_COLS = MLA_HEADS * QK_HEAD_DIM
Q_PAD_COLS = MLA_HEADS * Q_PAD_DIM
VMEM_LIMIT = 56 * 1024 * 1024
NT_DIMS = (((1,), (1,)), ((), ()))

TM_PROJ, TN_PROJ = 1024, 1024
TM_KV = 512
TQ, TK = 1024, 1024
assert TQ == TK and TK % TM_KV == 0
Q_CHUNKS = 2
PROBE_KEYS = BF16_ROWS
TS_LRU, CB_LRU = 512, 256
TM_MERGE = 256
TM_CROSS = 512
TM_UP, TN_UP, HALO_UP = 1024, 512, 16
TM_DOWN = 256


def _params(*sem):
    return pltpu.CompilerParams(dimension_semantics=sem, vmem_limit_bytes=VMEM_LIMIT)


def _resident(shape, index_map):
    return pl.BlockSpec(shape, index_map, pipeline_mode=pl.Buffered(1))


def _rms(x, g):
    return x * lax.rsqrt(jnp.mean(x * x, axis=-1, keepdims=True) + EPS) * g


def _gelu_tanh(x):
    c = math.sqrt(2.0 / math.pi)
    return 0.5 * x * (1.0 + jnp.tanh(c * (x + 0.044715 * (x * x * x))))


def _rope_slab(slab, trig_ref):
    c = trig_ref[:, 0:LANES]
    s1 = trig_ref[:, LANES:2 * LANES]
    s2 = trig_ref[:, 2 * LANES:3 * LANES]
    return slab * c + pltpu.roll(slab, 96, 1) * s1 + pltpu.roll(slab, 32, 1) * s2


def _proj_kernel(x_ref, g_ref, wq_ref, wr_ref, trig_ref, o_ref, h_ref, *, n_q_tiles):
    j = pl.program_id(1)

    @pl.when(j == 0)
    def _():
        h_ref[...] = _rms(x_ref[...], g_ref[...]).astype(BF16)

    @pl.when(j >= n_q_tiles)
    def _():
        o_ref[...] = lax.dot_general(h_ref[...], wr_ref[...], NT_DIMS,
                                     preferred_element_type=F32).astype(BF16)

    @pl.when(j < n_q_tiles)
    def _():
        acc = lax.dot_general(h_ref[...], wq_ref[...], NT_DIMS, preferred_element_type=F32)
        for hh in range(TN_PROJ // Q_PAD_DIM):
            c0 = hh * Q_PAD_DIM
            o_ref[:, c0:c0 + LANES] = acc[:, c0:c0 + LANES].astype(BF16)
            slab = acc[:, c0 + LANES:c0 + Q_PAD_DIM]
            o_ref[:, c0 + LANES:c0 + Q_PAD_DIM] = _rope_slab(slab, trig_ref).astype(BF16)


def _proj_in(x2d, g, wq_t, wr_t, trig):
    t = x2d.shape[0]
    nq = wq_t.shape[0] // TN_PROJ
    n = wq_t.shape[0] + wr_t.shape[0]
    return pl.pallas_call(
        functools.partial(_proj_kernel, n_q_tiles=nq),
        grid=(t // TM_PROJ, n // TN_PROJ),
        in_specs=[
            pl.BlockSpec((TM_PROJ, D_MODEL), lambda i, j: (i, 0)),
            pl.BlockSpec((1, D_MODEL), lambda i, j: (0, 0)),
            pl.BlockSpec((TN_PROJ, D_MODEL), lambda i, j: (jnp.minimum(j, nq - 1), 0)),
            pl.BlockSpec((TN_PROJ, D_MODEL), lambda i, j: (jnp.maximum(j - nq, 0), 0)),
            pl.BlockSpec((TM_PROJ, 3 * LANES), lambda i, j: (i, 0)),
        ],
        out_specs=pl.BlockSpec((TM_PROJ, TN_PROJ), lambda i, j: (i, j)),
        out_shape=jax.ShapeDtypeStruct((t, n), BF16),
        scratch_shapes=[pltpu.VMEM((TM_PROJ, D_MODEL), BF16)],
        compiler_params=_params("arbitrary", "arbitrary"),
        name="proj_in",
    )(x2d, g, wq_t, wr_t, trig)


def _kv_kernel(x_ref, g_ref, wb_ref, gc_ref, wuk_ref, wuvt_ref, trig_ref,
               kn_ref, kr_ref, vt_ref):
    h = _rms(x_ref[...], g_ref[...]).astype(BF16)
    c = lax.dot_general(h, wb_ref[...], NT_DIMS, preferred_element_type=F32)
    n = _rms(c[:, :KV_LORA_RANK], gc_ref[...]).astype(BF16)
    kn_ref[...] = jnp.dot(n, wuk_ref[...], preferred_element_type=F32).astype(BF16)
    vt_ref[...] = lax.dot_general(wuvt_ref[...], n, NT_DIMS,
                                  preferred_element_type=F32).astype(BF16)
    kr_ref[...] = _rope_slab(c[:, KV_LORA_RANK:], trig_ref).astype(BF16)


def _kv_path(x2d, g, w_b, g_ckv, w_uk, w_uvt, trig, batch, seq):
    t = x2d.shape[0]
    nk = seq // TM_KV
    hv = MLA_HEADS * V_HEAD_DIM
    return pl.pallas_call(
        _kv_kernel,
        grid=(t // TM_KV,),
        in_specs=[
            pl.BlockSpec((TM_KV, D_MODEL), lambda i: (i, 0)),
            pl.BlockSpec((1, D_MODEL), lambda i: (0, 0)),
            _resident(w_b.shape, lambda i: (0, 0)),
            pl.BlockSpec((1, KV_LORA_RANK), lambda i: (0, 0)),
            _resident(w_uk.shape, lambda i: (0, 0)),
            _resident(w_uvt.shape, lambda i: (0, 0)),
            pl.BlockSpec((TM_KV, 3 * LANES), lambda i: (i, 0)),
        ],
        out_specs=[
            pl.BlockSpec((TM_KV, hv), lambda i: (i, 0)),
            pl.BlockSpec((TM_KV, LANES), lambda i: (i, 0)),
            pl.BlockSpec((None, hv, TM_KV), lambda i: (i, 0, 0)),
        ],
        out_shape=[
            jax.ShapeDtypeStruct((t, hv), BF16),
            jax.ShapeDtypeStruct((t, LANES), BF16),
            jax.ShapeDtypeStruct((batch * nk, hv, TM_KV), BF16),
        ],
        compiler_params=_params("arbitrary"),
        name="kv_path",
    )(x2d, g, w_b, g_ckv, w_uk, w_uvt, trig)


def _attn_kernel(q_ref, kn_ref, kr_ref, vt_ref, o_ref,
                 kcat_ref, vaug_ref, qt_ref, p0, p1, a0, a1, acc_ref, r_ref, rprev_ref):
    i = pl.program_id(2)
    n = i + 1

    @pl.when(i == 0)
    def _():
        kcat_ref[:, 0:LANES] = kn_ref[...]
        kcat_ref[:, LANES:Q_PAD_DIM] = kr_ref[...]
        sub = TK // TM_KV
        for j in range(vaug_ref.shape[0]):
            for c in range(sub):
                vaug_ref[j, 0:V_HEAD_DIM, c * TM_KV:(c + 1) * TM_KV] = vt_ref[j * sub + c]
        vaug_ref[:, V_HEAD_DIM:V_AUG_DIM, :] = jnp.ones(
            (vaug_ref.shape[0], V_AUG_DIM - V_HEAD_DIM, TK), BF16)

    qt_ref[...] = q_ref[...].astype(F32).T.astype(BF16)
    acc_ref[...] = jnp.zeros(acc_ref.shape, F32)

    s_probe = jnp.dot(kcat_ref[0:PROBE_KEYS, :], qt_ref[...], preferred_element_type=F32)
    key = lax.broadcasted_iota(jnp.int32, s_probe.shape, 0)
    qry = lax.broadcasted_iota(jnp.int32, s_probe.shape, 1) + i * TQ
    r0 = jnp.max(jnp.where(key <= qry, s_probe, -jnp.inf), axis=0, keepdims=True)
    r_ref[...] = r0
    rprev_ref[...] = r0

    def qs(b, p_ref, a_ref, diagonal, cols):
        k0 = pl.multiple_of(b * TK, TK)
        s = jnp.dot(kcat_ref[pl.ds(k0, TK), :], qt_ref[:, cols],
                    preferred_element_type=F32)
        if diagonal:
            key = lax.broadcasted_iota(jnp.int32, s.shape, 0)
            qry = lax.broadcasted_iota(jnp.int32, s.shape, 1) + cols.start
            s = jnp.where(key <= qry, s, -jnp.inf)
        r = r_ref[:, cols]
        p_ref[:, cols] = jnp.exp2(s - r).astype(BF16)
        a_ref[:, cols] = jnp.exp2(rprev_ref[:, cols] - r)
        rprev_ref[:, cols] = r
        r_ref[:, cols] = jnp.maximum(r, jnp.max(s, axis=0, keepdims=True))

    def pv(b, p_ref, a_ref, cols):
        acc_ref[:, cols] = acc_ref[:, cols] * a_ref[:, cols] + jnp.dot(
            vaug_ref[b], p_ref[:, cols], preferred_element_type=F32)

    bufs = ((p0, a0), (p1, a1))

    def step(t, parity, do_qs=False, diagonal=False, do_pv=False):
        p_c, a_c = bufs[parity]
        p_p, a_p = bufs[1 - parity]
        for c in range(Q_CHUNKS):
            cols = slice(c * (TQ // Q_CHUNKS), (c + 1) * (TQ // Q_CHUNKS))
            if do_qs:
                qs(t, p_c, a_c, diagonal, cols)
            if do_pv:
                pv(t - 1, p_p, a_p, cols)

    def by_parity(t, cond, **stages):
        for parity in (0, 1):
            @pl.when(jnp.logical_and(cond, t % 2 == parity))
            def _():
                step(t, parity, **stages)

    @pl.when(n == 1)
    def _():
        step(0, 0, do_qs=True, diagonal=True)

    @pl.when(n >= 2)
    def _():
        step(0, 0, do_qs=True)

    def steady(t, carry):
        by_parity(t, True, do_qs=True, do_pv=True)
        return carry

    lax.fori_loop(1, n - 1, steady, 0)
    by_parity(n - 1, n >= 2, do_qs=True, diagonal=True, do_pv=True)
    by_parity(n, True, do_pv=True)

    out = acc_ref[0:V_HEAD_DIM, :] * (1.0 / acc_ref[V_HEAD_DIM:V_HEAD_DIM + 1, :])
    o_ref[...] = out.T.astype(BF16)


def _mla_attn(proj3, kn3, kr3, vt4, batch, seq):
    nk = seq // TK
    hv = MLA_HEADS * V_HEAD_DIM
    return pl.pallas_call(
        _attn_kernel,
        grid=(batch, MLA_HEADS, seq // TQ),
        in_specs=[
            pl.BlockSpec((None, TQ, Q_PAD_DIM), lambda b, h, i: (b, i, h)),
            pl.BlockSpec((None, seq, LANES), lambda b, h, i: (b, 0, h)),
            pl.BlockSpec((None, seq, LANES), lambda b, h, i: (b, 0, 0)),
            pl.BlockSpec((None, seq // TM_KV, V_HEAD_DIM, TM_KV), lambda b, h, i: (b, 0, h, 0)),
        ],
        out_specs=pl.BlockSpec((None, TQ, V_HEAD_DIM), lambda b, h, i: (b, i, h)),
        out_shape=jax.ShapeDtypeStruct((batch, seq, hv), BF16),
        scratch_shapes=[
            pltpu.VMEM((seq, Q_PAD_DIM), BF16),
            pltpu.VMEM((nk, V_AUG_DIM, TK), BF16),
            pltpu.VMEM((Q_PAD_DIM, TQ), BF16),
            pltpu.VMEM((TK, TQ), BF16),
            pltpu.VMEM((TK, TQ), BF16),
            pltpu.VMEM((1, TQ), F32),
            pltpu.VMEM((1, TQ), F32),
            pltpu.VMEM((V_AUG_DIM, TQ), F32),
            pltpu.VMEM((1, TQ), F32),
            pltpu.VMEM((1, TQ), F32),
        ],
        compiler_params=_params("arbitrary", "arbitrary", "arbitrary"),
        name="mla_attn",
    )(proj3, kn3, kr3, vt4)


def _lru_kernel(x_ref, y_ref, wc_ref, bc_ref, wrg_ref, brg_ref, wig_ref, big_ref, lam_ref,
                o_ref, xprev, a_s, b_s, hc_ref):
    si = pl.program_id(2)
    ts, cb = TS_LRU, CB_LRU

    @pl.when(si == 0)
    def _():
        xprev[...] = jnp.zeros((8, cb), F32)
        hc_ref[...] = jnp.zeros((8, cb), F32)

    x = x_ref[...].astype(F32)
    prev = xprev[...]
    row8 = lax.broadcasted_iota(jnp.int32, (8, cb), 0)
    last = LRU_CONV_WIDTH - 1
    xc = bc_ref[...] + x * wc_ref[last:last + 1, :]
    for k in range(1, LRU_CONV_WIDTH):
        xs = pltpu.roll(x, k, 0)
        head = jnp.where(row8 < k, pltpu.roll(prev, k, 0), xs[0:8, :])
        xs = jnp.concatenate([head, xs[8:, :]], axis=0)
        xc = xc + xs * wc_ref[last - k:last - k + 1, :]
    xprev[...] = x[ts - 8:, :]

    xcb = xc.astype(BF16)
    r = jax.nn.sigmoid(jnp.dot(xcb, wrg_ref[...], preferred_element_type=F32) + brg_ref[...])
    ig = jax.nn.sigmoid(jnp.dot(xcb, wig_ref[...], preferred_element_type=F32) + big_ref[...])
    z = -lam_ref[...]
    softplus = jnp.maximum(z, 0.0) + jnp.log1p(jnp.exp(-jnp.abs(z)))
    a = jnp.exp((-LRU_C * softplus) * r)
    v = 1.0 - a * a
    u = jnp.where(v > 0.0, v * lax.rsqrt(v), 0.0) * (ig * xc)

    ng = ts // 8
    a = a.reshape(ng, 8, cb)
    u = u.reshape(ng, 8, cb)
    row = lax.broadcasted_iota(jnp.int32, (ng, 8, cb), 1)
    for d in (1, 2, 4):
        keep = row >= d
        a_sh = pltpu.roll(a, d, 1)
        u_sh = pltpu.roll(u, d, 1)
        u = jnp.where(keep, a, 0.0) * u_sh + u
        a = a * jnp.where(keep, a_sh, 1.0)
    a_s[...] = a.reshape(ts, cb)
    b_s[...] = u.reshape(ts, cb)

    hc = hc_ref[...]
    for g in range(ts // 8):
        hg = a_s[8 * g:8 * g + 8, :] * hc + b_s[8 * g:8 * g + 8, :]
        b_s[8 * g:8 * g + 8, :] = hg
        hc = jnp.broadcast_to(hg[7:8, :], (8, cb))
    hc_ref[...] = hc

    o_ref[...] = (b_s[...] * _gelu_tanh(y_ref[...].astype(F32))).astype(BF16)


def _lru(proj3, w_conv, b_conv, w_rg2, b_rg, w_ig2, b_ig, lam, batch, seq):
    ts, cb = TS_LRU, CB_LRU
    ncb = LRU_WIDTH // cb
    x_off = Q_PAD_COLS // cb
    y_off = x_off + ncb
    return pl.pallas_call(
        _lru_kernel,
        grid=(batch, ncb, seq // ts),
        in_specs=[
            pl.BlockSpec((None, ts, cb), lambda b, c, s: (b, s, x_off + c)),
            pl.BlockSpec((None, ts, cb), lambda b, c, s: (b, s, y_off + c)),
            pl.BlockSpec((LRU_CONV_WIDTH, cb), lambda b, c, s: (0, c)),
            pl.BlockSpec((1, cb), lambda b, c, s: (0, c)),
            pl.BlockSpec((None, cb, cb), lambda b, c, s: (c, 0, 0)),
            pl.BlockSpec((1, cb), lambda b, c, s: (0, c)),
            pl.BlockSpec((None, cb, cb), lambda b, c, s: (c, 0, 0)),
            pl.BlockSpec((1, cb), lambda b, c, s: (0, c)),
            pl.BlockSpec((1, cb), lambda b, c, s: (0, c)),
        ],
        out_specs=pl.BlockSpec((None, ts, cb), lambda b, c, s: (b, s, c)),
        out_shape=jax.ShapeDtypeStruct((batch, seq, LRU_WIDTH), BF16),
        scratch_shapes=[
            pltpu.VMEM((8, cb), F32),
            pltpu.VMEM((ts, cb), F32),
            pltpu.VMEM((ts, cb), F32),
            pltpu.VMEM((8, cb), F32),
        ],
        compiler_params=_params("arbitrary", "arbitrary", "arbitrary"),
        name="lru",
    )(proj3, proj3, w_conv, b_conv, w_rg2, b_rg, w_ig2, b_ig, lam)


def _merge_kernel(attn_ref, lru_ref, gm_ref, gl_ref, x_ref, wom_ref, wol_ref, wout_ref, g_ref,
                  o_ref):
    o_mla = jnp.dot(attn_ref[...], wom_ref[...], preferred_element_type=F32)
    o_lru = jnp.dot(lru_ref[...], wol_ref[...], preferred_element_type=F32)
    merged = (jax.nn.sigmoid(gm_ref[...].astype(F32)) * o_mla
              + jax.nn.sigmoid(gl_ref[...].astype(F32)) * o_lru)
    y = jnp.dot(merged.astype(BF16), wout_ref[...], preferred_element_type=F32)
    o_ref[...] = x_ref[...] + _rms(y, g_ref[...])


def _merge(attn2, lru2, proj2, x2d, w_om, w_ol, w_out, g_post):
    t = x2d.shape[0]
    tm = TM_MERGE
    gm_off = (Q_PAD_COLS + 2 * LRU_WIDTH) // D_MODEL
    row = lambda i: (i, 0)
    const = lambda i: (0, 0)
    return pl.pallas_call(
        _merge_kernel,
        grid=(t // tm,),
        in_specs=[
            pl.BlockSpec((tm, D_MODEL), row),
            pl.BlockSpec((tm, D_MODEL), row),
            pl.BlockSpec((tm, D_MODEL), lambda i: (i, gm_off)),
            pl.BlockSpec((tm, D_MODEL), lambda i: (i, gm_off + 1)),
            pl.BlockSpec((tm, D_MODEL), row),
            _resident((D_MODEL, D_MODEL), const),
            _resident((D_MODEL, D_MODEL), const),
            _resident((D_MODEL, D_MODEL), const),
            pl.BlockSpec((1, D_MODEL), const),
        ],
        out_specs=pl.BlockSpec((tm, D_MODEL), row),
        out_shape=jax.ShapeDtypeStruct((t, D_MODEL), F32),
        compiler_params=_params("arbitrary"),
        name="merge",
    )(attn2, lru2, proj2, proj2, x2d, w_om, w_ol, w_out, g_post)


def _memkv_kernel(mem_ref, g_ref, wckt_ref, wcv_ref, ckt_ref, cv_ref):
    mn = _rms(mem_ref[...], g_ref[...]).astype(BF16)
    ckt_ref[...] = lax.dot_general(wckt_ref[...], mn, (((1,), (1,)), ((), ())),
                                   preferred_element_type=F32).astype(BF16)
    cv_ref[...] = jnp.dot(mn, wcv_ref[...], preferred_element_type=F32).astype(BF16)


def _mem_kv(mem, g_mem, w_ckt, w_cv):
    b, m, _ = mem.shape
    xd = X_HEADS * X_HEAD_DIM
    return pl.pallas_call(
        _memkv_kernel,
        grid=(b,),
        in_specs=[
            pl.BlockSpec((None, m, D_MODEL), lambda i: (i, 0, 0)),
            pl.BlockSpec((1, D_MODEL), lambda i: (0, 0)),
            pl.BlockSpec((xd, D_MODEL), lambda i: (0, 0)),
            pl.BlockSpec((D_MODEL, xd), lambda i: (0, 0)),
        ],
        out_specs=[
            pl.BlockSpec((None, xd, m), lambda i: (i, 0, 0)),
            pl.BlockSpec((None, m, xd), lambda i: (i, 0, 0)),
        ],
        out_shape=[
            jax.ShapeDtypeStruct((b, xd, m), BF16),
            jax.ShapeDtypeStruct((b, m, xd), BF16),
        ],
        compiler_params=_params("arbitrary"),
        name="mem_kv",
    )(mem, g_mem, w_ckt, w_cv)


def _cross_kernel(x_ref, gpre_ref, wcq_ref, ckt_ref, cv_ref, wco_ref, gpost_ref, o_ref):
    x = x_ref[...]
    h = _rms(x, gpre_ref[...]).astype(BF16)
    cq = jnp.dot(h, wcq_ref[...], preferred_element_type=F32).astype(BF16)
    outs = []
    for hh in range(X_HEADS):
        sl = slice(hh * X_HEAD_DIM, (hh + 1) * X_HEAD_DIM)
        s = jnp.dot(cq[:, sl], ckt_ref[sl, :], preferred_element_type=F32)
        p = jnp.exp2(s - jnp.max(s, axis=-1, keepdims=True))
        l = jnp.sum(p, axis=-1, keepdims=True)
        pv = jnp.dot(p.astype(BF16), cv_ref[:, sl], preferred_element_type=F32)
        outs.append((pv * (1.0 / l)).astype(BF16))
    co = jnp.concatenate(outs, axis=-1)
    y = jnp.dot(co, wco_ref[...], preferred_element_type=F32)
    o_ref[...] = x + _rms(y, gpost_ref[...])


def _cross(x2d, g_pre, w_cq, ckt, cv, w_co, g_post, seq):
    t = x2d.shape[0]
    tm = TM_CROSS
    xd = X_HEADS * X_HEAD_DIM
    m = cv.shape[1]
    per_b = seq // tm
    return pl.pallas_call(
        _cross_kernel,
        grid=(t // tm,),
        in_specs=[
            pl.BlockSpec((tm, D_MODEL), lambda i: (i, 0)),
            pl.BlockSpec((1, D_MODEL), lambda i: (0, 0)),
            pl.BlockSpec((D_MODEL, xd), lambda i: (0, 0)),
            pl.BlockSpec((None, xd, m), lambda i: (i // per_b, 0, 0)),
            pl.BlockSpec((None, m, xd), lambda i: (i // per_b, 0, 0)),
            pl.BlockSpec((xd, D_MODEL), lambda i: (0, 0)),
            pl.BlockSpec((1, D_MODEL), lambda i: (0, 0)),
        ],
        out_specs=pl.BlockSpec((tm, D_MODEL), lambda i: (i, 0)),
        out_shape=jax.ShapeDtypeStruct((t, D_MODEL), F32),
        compiler_params=_params("arbitrary"),
        name="cross",
    )(x2d, g_pre, w_cq, ckt, cv, w_co, g_post)


def _up_kernel(x_ref, halo_ref, g_ref, wg_ref, wv_ref, cwg_ref, cwv_ref, cbg_ref, cbv_ref,
               o_ref, h_ref, ag_ref, av_ref, *, tiles_per_seq):
    i = pl.program_id(0)
    j = pl.program_id(1)
    tm, halo = TM_UP, HALO_UP

    @pl.when(j == 0)
    def _():
        g = g_ref[...]
        keep = jnp.where(i % tiles_per_seq == 0, 0.0, 1.0)
        h_ref[0:halo, :] = (_rms(halo_ref[...], g) * keep).astype(BF16)
        h_ref[halo:halo + tm, :] = _rms(x_ref[...], g).astype(BF16)

    h = h_ref[...]
    ag_ref[...] = jnp.dot(h, wg_ref[...], preferred_element_type=F32)
    av_ref[...] = jnp.dot(h, wv_ref[...], preferred_element_type=F32)

    def conv(a_ref, w_ref, b_ref):
        acc = b_ref[...]
        for k in range(FFN_CONV_WIDTH):
            acc = acc + a_ref[pl.ds(halo - (FFN_CONV_WIDTH - 1) + k, tm), :] * w_ref[k:k + 1, :]
        return acc

    gate = conv(ag_ref, cwg_ref, cbg_ref)
    val = conv(av_ref, cwv_ref, cbv_ref)
    o_ref[...] = (_gelu_tanh(gate) * val).astype(BF16)


def _ffn_up(x2d, g_pre, w_up, w_fconv, b_fconv, seq):
    t = x2d.shape[0]
    tm, tn, halo = TM_UP, TN_UP, HALO_UP
    nj = D_FF // tn
    return pl.pallas_call(
        functools.partial(_up_kernel, tiles_per_seq=seq // tm),
        grid=(t // tm, nj),
        in_specs=[
            pl.BlockSpec((tm, D_MODEL), lambda i, j: (i, 0)),
            pl.BlockSpec((halo, D_MODEL), lambda i, j: (jnp.maximum(i * (tm // halo) - 1, 0), 0)),
            pl.BlockSpec((1, D_MODEL), lambda i, j: (0, 0)),
            pl.BlockSpec((D_MODEL, tn), lambda i, j: (0, j)),
            pl.BlockSpec((D_MODEL, tn), lambda i, j: (0, j + nj)),
            pl.BlockSpec((FFN_CONV_WIDTH, tn), lambda i, j: (0, j)),
            pl.BlockSpec((FFN_CONV_WIDTH, tn), lambda i, j: (0, j + nj)),
            pl.BlockSpec((1, tn), lambda i, j: (0, j)),
            pl.BlockSpec((1, tn), lambda i, j: (0, j + nj)),
        ],
        out_specs=pl.BlockSpec((tm, tn), lambda i, j: (i, j)),
        out_shape=jax.ShapeDtypeStruct((t, D_FF), BF16),
        scratch_shapes=[
            pltpu.VMEM((tm + halo, D_MODEL), BF16),
            pltpu.VMEM((tm + halo, tn), F32),
            pltpu.VMEM((tm + halo, tn), F32),
        ],
        compiler_params=_params("arbitrary", "arbitrary"),
        name="ffn_up",
    )(x2d, x2d, g_pre, w_up, w_up, w_fconv, w_fconv, b_fconv, b_fconv)


def _down_kernel(a_ref, x_ref, w_ref, g_ref, o_ref):
    y = jnp.dot(a_ref[...], w_ref[...], preferred_element_type=F32)
    o_ref[...] = x_ref[...] + _rms(y, g_ref[...])


def _ffn_down(act, x2d, w_down, g_post):
    t = x2d.shape[0]
    tm = TM_DOWN
    return pl.pallas_call(
        _down_kernel,
        grid=(t // tm,),
        in_specs=[
            pl.BlockSpec((tm, D_FF), lambda i: (i, 0)),
            pl.BlockSpec((tm, D_MODEL), lambda i: (i, 0)),
            _resident((D_FF, D_MODEL), lambda i: (0, 0)),
            pl.BlockSpec((1, D_MODEL), lambda i: (0, 0)),
        ],
        out_specs=pl.BlockSpec((tm, D_MODEL), lambda i: (i, 0)),
        out_shape=jax.ShapeDtypeStruct((t, D_MODEL), F32),
        compiler_params=_params("arbitrary"),
        name="ffn_down",
    )(act, x2d, w_down, g_post)


def _rope_table(positions):
    half = QK_ROPE_DIM // 2
    inv_freq = ROPE_THETA ** (-jnp.arange(0, QK_ROPE_DIM, 2, dtype=F32) / QK_ROPE_DIM)
    ang = positions.astype(F32).reshape(-1, 1) * inv_freq
    cos, sin = jnp.cos(ang), jnp.sin(ang)
    z = jnp.zeros_like(cos)
    assert half * 4 == LANES
    return jnp.concatenate([cos, cos, z, z, -sin, z, z, z, z, sin, z, z], axis=-1)


def _block_diag_pairs(w):
    nb, d, _ = w.shape
    w = w.reshape(nb // 2, 2, d, d)
    z = jnp.zeros_like(w[:, 0])
    top = jnp.concatenate([w[:, 0], z], axis=-1)
    bot = jnp.concatenate([z, w[:, 1]], axis=-1)
    return jnp.concatenate([top, bot], axis=-2)


def kernel(x, mem, positions, g_pre_mix, g_post_mix, w_in, g_ckv, w_ukv, w_o_mla, w_conv_lru, b_conv_lru, w_rg, b_rg, w_ig, b_ig, lru_lambda, w_o_lru, w_out, g_pre_x, g_post_x, g_mem, w_cq, w_ck, w_cv, w_co, g_pre_ffn, g_post_ffn, w_up, w_fconv, b_fconv, w_down):
    batch, seq, d = x.shape
    assert d == D_MODEL and g_pre_mix.shape[0] == 1
    t = batch * seq
    x2d = x.reshape(t, d)
    l = 0

    q_scale = (QK_HEAD_DIM ** -0.5) * LOG2E
    wi_t = w_in[l].T
    wq_t = wi_t[:Q_COLS].reshape(MLA_HEADS, QK_HEAD_DIM, d) * q_scale
    wq_t = jnp.pad(wq_t, ((0, 0), (0, Q_PAD_DIM - QK_HEAD_DIM), (0, 0)))
    wq_t = wq_t.reshape(Q_PAD_COLS, d).astype(BF16)
    kv_end = Q_COLS + KV_LORA_RANK + QK_ROPE_DIM
    wr_t = wi_t[kv_end:].astype(BF16)
    wb_t = jnp.pad(wi_t[Q_COLS:kv_end], ((0, LANES - QK_ROPE_DIM), (0, 0))).astype(BF16)
    wukv = w_ukv[l].reshape(KV_LORA_RANK, MLA_HEADS, QK_NOPE_DIM + V_HEAD_DIM)
    w_uk = wukv[:, :, :QK_NOPE_DIM].reshape(KV_LORA_RANK, -1).astype(BF16)
    w_uvt = wukv[:, :, QK_NOPE_DIM:].reshape(KV_LORA_RANK, -1).T.astype(BF16)
    row = lambda v: v.reshape(1, -1)
    trig = _rope_table(positions)

    proj = _proj_in(x2d, row(g_pre_mix[l]), wq_t, wr_t, trig)
    kn, kr, vt = _kv_path(x2d, row(g_pre_mix[l]), wb_t, row(g_ckv[l]), w_uk, w_uvt, trig,
                          batch, seq)
    proj3 = proj.reshape(batch, seq, -1)
    attn = _mla_attn(proj3, kn.reshape(batch, seq, -1), kr.reshape(batch, seq, LANES),
                     vt.reshape(batch, seq // TM_KV, MLA_HEADS * V_HEAD_DIM, TM_KV), batch, seq)
    lru = _lru(proj3, w_conv_lru[l], row(b_conv_lru[l]),
               _block_diag_pairs(w_rg[l]).astype(BF16), row(b_rg[l]),
               _block_diag_pairs(w_ig[l]).astype(BF16), row(b_ig[l]),
               row(lru_lambda[l]), batch, seq)
    x1 = _merge(attn.reshape(t, -1), lru.reshape(t, -1), proj, x2d,
                w_o_mla[l].astype(BF16), w_o_lru[l].astype(BF16), w_out[l].astype(BF16),
                row(g_post_mix[l]))

    c_scale = (X_HEAD_DIM ** -0.5) * LOG2E
    ckt, cv = _mem_kv(mem, row(g_mem[l]), w_ck[l].T.astype(BF16), w_cv[l].astype(BF16))
    x2 = _cross(x1, row(g_pre_x[l]), (w_cq[l] * c_scale).astype(BF16), ckt, cv,
                w_co[l].astype(BF16), row(g_post_x[l]), seq)

    act = _ffn_up(x2, row(g_pre_ffn[l]), w_up[l].astype(BF16), w_fconv[l], row(b_fconv[l]), seq)
    x3 = _ffn_down(act, x2, w_down[l].astype(BF16), row(g_post_ffn[l]))
    return x3.reshape(batch, seq, d)
```

```python
import functools
import math

import jax
import jax.numpy as jnp
from jax import lax
from jax.experimental import pallas as pl
from jax.experimental.pallas import tpu as pltpu

F32 = jnp.float32
BF16 = jnp.bfloat16

D_MODEL = 2048
MLA_HEADS = 16
QK_NOPE_DIM = 128
QK_ROPE_DIM = 64
V_HEAD_DIM = 128
KV_LORA_RANK = 512
QK_HEAD_DIM = QK_NOPE_DIM + QK_ROPE_DIM
ROPE_THETA = 10000.0
LRU_WIDTH = D_MODEL
LRU_BLOCKS = 16
LRU_BLOCK_DIM = LRU_WIDTH // LRU_BLOCKS
LRU_CONV_WIDTH = 4
LRU_C = 8.0
X_HEADS = 4
X_HEAD_DIM = 128
D_FF = 5632
FFN_CONV_WIDTH = 3
EPS = 1e-6

LOG2E = math.log2(math.e)
LANES = 128
BF16_ROWS = 16
Q_PAD_DIM = 2 * LANES
V_AUG_DIM = V_HEAD_DIM + BF16_ROWS
Q_COLS = MLA_HEADS * QK_HEAD_DIM
VMEM_LIMIT = 56 * 1024 * 1024
NT_DIMS = (((1,), (1,)), ((), ()))

TM_PROJ, TN_PROJ = 1024, 1024
TN_Q = 4 * QK_HEAD_DIM
TM_KV = 512
TQ, TK = 1024, 1024
assert TQ == TK and TK % TM_KV == 0
Q_CHUNKS = 2
PROBE_KEYS = BF16_ROWS
SCORE_JUMP_LIMIT = 64.0
TS_LRU, CB_LRU = 512, 256
TM_MERGE = 256
TM_CROSS = 512
TM_UP, TN_UP, HALO_UP = 1024, 512, 16
TM_DOWN = 256


def _params(*sem):
    return pltpu.CompilerParams(dimension_semantics=sem, vmem_limit_bytes=VMEM_LIMIT)


def _resident(shape, index_map):
    return pl.BlockSpec(shape, index_map, pipeline_mode=pl.Buffered(1))


def _rms(x, g):
    return x * lax.rsqrt(jnp.mean(x * x, axis=-1, keepdims=True) + EPS) * g


def _gelu_tanh(x):
    c = math.sqrt(2.0 / math.pi)
    return 0.5 * x * (1.0 + jnp.tanh(c * (x + 0.044715 * (x * x * x))))


def _rope_tables(cos_ref, sin_ref):
    c = cos_ref[...]
    s = sin_ref[...]
    z = jnp.zeros_like(c)
    return jnp.concatenate([c, c, z, z, -s, z, z, z, z, s, z, z], axis=-1)


def _rope_slab(slab, trig):
    c = trig[:, 0:LANES]
    s1 = trig[:, LANES:2 * LANES]
    s2 = trig[:, 2 * LANES:3 * LANES]
    return slab * c + pltpu.roll(slab, 96, 1) * s1 + pltpu.roll(slab, 32, 1) * s2


def _proj_kernel(x_ref, g_ref, wq_ref, wr_ref, cos_ref, sin_ref, qt_ref, o_ref, h_ref, *,
                 n_q_tiles):
    j = pl.program_id(1)

    @pl.when(j == 0)
    def _():
        h_ref[...] = _rms(x_ref[...], g_ref[...]).astype(BF16)

    @pl.when(j >= n_q_tiles)
    def _():
        o_ref[...] = lax.dot_general(h_ref[...], wr_ref[...], NT_DIMS,
                                     preferred_element_type=F32).astype(BF16)

    @pl.when(j < n_q_tiles)
    def _():
        acc = lax.dot_general(wq_ref[...], h_ref[...], NT_DIMS, preferred_element_type=F32)
        cos_t = cos_ref[...]
        sin_t = sin_ref[...]
        half = QK_ROPE_DIM // 2
        for hh in range(TN_Q // QK_HEAD_DIM):
            r0 = hh * QK_HEAD_DIM
            r1 = r0 + QK_NOPE_DIM
            qt_ref[r0:r1, :] = acc[r0:r1, :].astype(BF16)
            x1 = acc[r1:r1 + half, :]
            x2 = acc[r1 + half:r1 + 2 * half, :]
            qt_ref[r1:r1 + half, :] = (x1 * cos_t - x2 * sin_t).astype(BF16)
            qt_ref[r1 + half:r1 + 2 * half, :] = (x2 * cos_t + x1 * sin_t).astype(BF16)


def _proj_in(x2d, g, wq_t, wr_t, cos_t, sin_t, batch, seq):
    t = x2d.shape[0]
    nq = wq_t.shape[0] // TN_Q
    nr = wr_t.shape[0] // TN_PROJ
    per_b = seq // TM_PROJ
    half = QK_ROPE_DIM // 2
    return pl.pallas_call(
        functools.partial(_proj_kernel, n_q_tiles=nq),
        grid=(t // TM_PROJ, nq + nr),
        in_specs=[
            pl.BlockSpec((TM_PROJ, D_MODEL), lambda i, j: (i, 0)),
            pl.BlockSpec((1, D_MODEL), lambda i, j: (0, 0)),
            pl.BlockSpec((TN_Q, D_MODEL), lambda i, j: (jnp.minimum(j, nq - 1), 0)),
            pl.BlockSpec((TN_PROJ, D_MODEL), lambda i, j: (jnp.maximum(j - nq, 0), 0)),
            pl.BlockSpec((half, TM_PROJ), lambda i, j: (0, i)),
            pl.BlockSpec((half, TM_PROJ), lambda i, j: (0, i)),
        ],
        out_specs=[
            pl.BlockSpec((None, TN_Q, TM_PROJ),
                         lambda i, j: (i // per_b, jnp.minimum(j, nq - 1), i % per_b)),
            pl.BlockSpec((TM_PROJ, TN_PROJ), lambda i, j: (i, jnp.maximum(j - nq, 0))),
        ],
        out_shape=[
            jax.ShapeDtypeStruct((batch, wq_t.shape[0], seq), BF16),
            jax.ShapeDtypeStruct((t, wr_t.shape[0]), BF16),
        ],
        scratch_shapes=[pltpu.VMEM((TM_PROJ, D_MODEL), BF16)],
        compiler_params=_params("arbitrary", "arbitrary"),
        name="proj_in",
    )(x2d, g, wq_t, wr_t, cos_t, sin_t)


def _kv_kernel(x_ref, g_ref, wb_ref, gc_ref, wuk_ref, wuvt_ref, cos_ref, sin_ref,
               kn_ref, kr_ref, vt_ref):
    h = _rms(x_ref[...], g_ref[...]).astype(BF16)
    c = lax.dot_general(h, wb_ref[...], NT_DIMS, preferred_element_type=F32)
    n = _rms(c[:, :KV_LORA_RANK], gc_ref[...]).astype(BF16)
    kn_ref[...] = jnp.dot(n, wuk_ref[...], preferred_element_type=F32).astype(BF16)
    vt_ref[...] = lax.dot_general(wuvt_ref[...], n, NT_DIMS,
                                  preferred_element_type=F32).astype(BF16)
    kr_ref[...] = _rope_slab(c[:, KV_LORA_RANK:], _rope_tables(cos_ref, sin_ref)).astype(BF16)


def _kv_path(x2d, g, w_b, g_ckv, w_uk, w_uvt, cos, sin, batch, seq):
    t = x2d.shape[0]
    nk = seq // TM_KV
    hv = MLA_HEADS * V_HEAD_DIM
    return pl.pallas_call(
        _kv_kernel,
        grid=(t // TM_KV,),
        in_specs=[
            pl.BlockSpec((TM_KV, D_MODEL), lambda i: (i, 0)),
            pl.BlockSpec((1, D_MODEL), lambda i: (0, 0)),
            _resident(w_b.shape, lambda i: (0, 0)),
            pl.BlockSpec((1, KV_LORA_RANK), lambda i: (0, 0)),
            _resident(w_uk.shape, lambda i: (0, 0)),
            _resident(w_uvt.shape, lambda i: (0, 0)),
            pl.BlockSpec((TM_KV, QK_ROPE_DIM // 2), lambda i: (i, 0)),
            pl.BlockSpec((TM_KV, QK_ROPE_DIM // 2), lambda i: (i, 0)),
        ],
        out_specs=[
            pl.BlockSpec((TM_KV, hv), lambda i: (i, 0)),
            pl.BlockSpec((TM_KV, LANES), lambda i: (i, 0)),
            pl.BlockSpec((None, hv, TM_KV), lambda i: (i, 0, 0)),
        ],
        out_shape=[
            jax.ShapeDtypeStruct((t, hv), BF16),
            jax.ShapeDtypeStruct((t, LANES), BF16),
            jax.ShapeDtypeStruct((batch * nk, hv, TM_KV), BF16),
        ],
        compiler_params=_params("arbitrary"),
        name="kv_path",
    )(x2d, g, w_b, g_ckv, w_uk, w_uvt, cos, sin)


def _attn_kernel(q_ref, kn_ref, kr_ref, vt_ref, o_ref,
                 kcat_ref, vaug_ref, qt_ref, p0, p1, a0, a1, acc_ref, r_ref, rprev_ref, jump_ref):
    i = pl.program_id(2)
    n = i + 1

    @pl.when(i == 0)
    def _():
        kcat_ref[:, 0:LANES] = kn_ref[...]
        kcat_ref[:, LANES:Q_PAD_DIM] = kr_ref[...]
        sub = TK // TM_KV
        for j in range(vaug_ref.shape[0]):
            for c in range(sub):
                vaug_ref[j, 0:V_HEAD_DIM, c * TM_KV:(c + 1) * TM_KV] = vt_ref[j * sub + c]
        vaug_ref[:, V_HEAD_DIM:V_AUG_DIM, :] = jnp.ones(
            (vaug_ref.shape[0], V_AUG_DIM - V_HEAD_DIM, TK), BF16)
        qt_ref[QK_HEAD_DIM:Q_PAD_DIM, :] = jnp.zeros((Q_PAD_DIM - QK_HEAD_DIM, TQ), BF16)

    qt_ref[0:QK_HEAD_DIM, :] = q_ref[...]
    acc_ref[...] = jnp.zeros(acc_ref.shape, F32)

    s_probe = jnp.dot(kcat_ref[0:PROBE_KEYS, :], qt_ref[...], preferred_element_type=F32)
    key = lax.broadcasted_iota(jnp.int32, s_probe.shape, 0)
    qry = lax.broadcasted_iota(jnp.int32, s_probe.shape, 1) + i * TQ
    r0 = jnp.max(jnp.where(key <= qry, s_probe, -jnp.inf), axis=0, keepdims=True)
    r_ref[...] = r0
    rprev_ref[...] = r0
    jump_ref[...] = jnp.zeros(jump_ref.shape, F32)

    def qs(b, p_ref, a_ref, diagonal, cols):
        k0 = pl.multiple_of(b * TK, TK)
        s = jnp.dot(kcat_ref[pl.ds(k0, TK), :], qt_ref[:, cols],
                    preferred_element_type=F32)
        if diagonal:
            key = lax.broadcasted_iota(jnp.int32, s.shape, 0)
            qry = lax.broadcasted_iota(jnp.int32, s.shape, 1) + cols.start
            s = jnp.where(key <= qry, s, -jnp.inf)
        r = r_ref[:, cols]
        p_ref[:, cols] = jnp.exp2(s - r).astype(BF16)
        a_ref[:, cols] = jnp.exp2(rprev_ref[:, cols] - r)
        rprev_ref[:, cols] = r
        block_max = jnp.max(s, axis=0, keepdims=True)
        r_ref[:, cols] = jnp.maximum(r, block_max)
        jump_ref[:, cols] = jnp.maximum(jump_ref[:, cols], block_max - r)

    def pv(b, p_ref, a_ref, cols):
        acc_ref[:, cols] = acc_ref[:, cols] * a_ref[:, cols] + jnp.dot(
            vaug_ref[b], p_ref[:, cols], preferred_element_type=F32)

    bufs = ((p0, a0), (p1, a1))

    def step(t, parity, do_qs=False, diagonal=False, do_pv=False):
        p_c, a_c = bufs[parity]
        p_p, a_p = bufs[1 - parity]
        for c in range(Q_CHUNKS):
            cols = slice(c * (TQ // Q_CHUNKS), (c + 1) * (TQ // Q_CHUNKS))
            if do_qs:
                qs(t, p_c, a_c, diagonal, cols)
            if do_pv:
                pv(t - 1, p_p, a_p, cols)

    def by_parity(t, cond, **stages):
        for parity in (0, 1):
            @pl.when(jnp.logical_and(cond, t % 2 == parity))
            def _():
                step(t, parity, **stages)

    @pl.when(n == 1)
    def _():
        step(0, 0, do_qs=True, diagonal=True)

    @pl.when(n >= 2)
    def _():
        step(0, 0, do_qs=True)

    def steady(t, carry):
        by_parity(t, True, do_qs=True, do_pv=True)
        return carry

    lax.fori_loop(1, n - 1, steady, 0)
    by_parity(n - 1, n >= 2, do_qs=True, diagonal=True, do_pv=True)
    by_parity(n, True, do_pv=True)

    @pl.when(jnp.max(jump_ref[...]) > SCORE_JUMP_LIMIT)
    def _():
        acc_ref[...] = jnp.zeros(acc_ref.shape, F32)

        def block(b, m_old):
            k0 = pl.multiple_of(b * TK, TK)
            s = jnp.dot(kcat_ref[pl.ds(k0, TK), :], qt_ref[...], preferred_element_type=F32)
            key = lax.broadcasted_iota(jnp.int32, s.shape, 0) + b * TK
            qry = lax.broadcasted_iota(jnp.int32, s.shape, 1) + i * TQ
            s = jnp.where(key <= qry, s, -jnp.inf)
            m_new = jnp.maximum(m_old, jnp.max(s, axis=0, keepdims=True))
            p = jnp.exp2(s - m_new).astype(BF16)
            acc_ref[...] = acc_ref[...] * jnp.exp2(m_old - m_new) + jnp.dot(
                vaug_ref[b], p, preferred_element_type=F32)
            return m_new

        lax.fori_loop(0, n, block, jnp.full((1, TQ), -jnp.inf, F32))

    out = acc_ref[0:V_HEAD_DIM, :] * (1.0 / acc_ref[V_HEAD_DIM:V_HEAD_DIM + 1, :])
    o_ref[...] = out.T.astype(BF16)


def _mla_attn(q_t, kn3, kr3, vt4, batch, seq):
    nk = seq // TK
    hv = MLA_HEADS * V_HEAD_DIM
    return pl.pallas_call(
        _attn_kernel,
        grid=(batch, MLA_HEADS, seq // TQ),
        in_specs=[
            pl.BlockSpec((None, QK_HEAD_DIM, TQ), lambda b, h, i: (b, h, i)),
            pl.BlockSpec((None, seq, LANES), lambda b, h, i: (b, 0, h)),
            pl.BlockSpec((None, seq, LANES), lambda b, h, i: (b, 0, 0)),
            pl.BlockSpec((None, seq // TM_KV, V_HEAD_DIM, TM_KV), lambda b, h, i: (b, 0, h, 0)),
        ],
        out_specs=pl.BlockSpec((None, TQ, V_HEAD_DIM), lambda b, h, i: (b, i, h)),
        out_shape=jax.ShapeDtypeStruct((batch, seq, hv), BF16),
        scratch_shapes=[
            pltpu.VMEM((seq, Q_PAD_DIM), BF16),
            pltpu.VMEM((nk, V_AUG_DIM, TK), BF16),
            pltpu.VMEM((Q_PAD_DIM, TQ), BF16),
            pltpu.VMEM((TK, TQ), BF16),
            pltpu.VMEM((TK, TQ), BF16),
            pltpu.VMEM((1, TQ), F32),
            pltpu.VMEM((1, TQ), F32),
            pltpu.VMEM((V_AUG_DIM, TQ), F32),
            pltpu.VMEM((1, TQ), F32),
            pltpu.VMEM((1, TQ), F32),
            pltpu.VMEM((1, TQ), F32),
        ],
        compiler_params=_params("arbitrary", "arbitrary", "arbitrary"),
        name="mla_attn",
    )(q_t, kn3, kr3, vt4)


def _lru_kernel(x_ref, y_ref, wc_ref, bc_ref, wrg_ref, brg_ref, wig_ref, big_ref, lam_ref,
                o_ref, xprev, a_s, b_s, hc_ref):
    si = pl.program_id(2)
    ts, cb = TS_LRU, CB_LRU

    @pl.when(si == 0)
    def _():
        xprev[...] = jnp.zeros((8, cb), F32)
        hc_ref[...] = jnp.zeros((8, cb), F32)

    x = x_ref[...].astype(F32)
    prev = xprev[...]
    row8 = lax.broadcasted_iota(jnp.int32, (8, cb), 0)
    last = LRU_CONV_WIDTH - 1
    xc = bc_ref[...] + x * wc_ref[last:last + 1, :]
    for k in range(1, LRU_CONV_WIDTH):
        xs = pltpu.roll(x, k, 0)
        head = jnp.where(row8 < k, pltpu.roll(prev, k, 0), xs[0:8, :])
        xs = jnp.concatenate([head, xs[8:, :]], axis=0)
        xc = xc + xs * wc_ref[last - k:last - k + 1, :]
    xprev[...] = x[ts - 8:, :]

    xcb = xc.astype(BF16)
    r = jax.nn.sigmoid(jnp.dot(xcb, wrg_ref[...], preferred_element_type=F32) + brg_ref[...])
    ig = jax.nn.sigmoid(jnp.dot(xcb, wig_ref[...], preferred_element_type=F32) + big_ref[...])
    z = -lam_ref[...]
    softplus = jnp.maximum(z, 0.0) + jnp.log1p(jnp.exp(-jnp.abs(z)))
    a = jnp.exp((-LRU_C * softplus) * r)
    v = 1.0 - a * a
    u = jnp.where(v > 0.0, v * lax.rsqrt(v), 0.0) * (ig * xc)

    ng = ts // 8
    a = a.reshape(ng, 8, cb)
    u = u.reshape(ng, 8, cb)
    row = lax.broadcasted_iota(jnp.int32, (ng, 8, cb), 1)
    for d in (1, 2, 4):
        keep = row >= d
        a_sh = pltpu.roll(a, d, 1)
        u_sh = pltpu.roll(u, d, 1)
        u = jnp.where(keep, a, 0.0) * u_sh + u
        a = a * jnp.where(keep, a_sh, 1.0)
    a_s[...] = a.reshape(ts, cb)
    b_s[...] = u.reshape(ts, cb)

    hc = hc_ref[...]
    for g in range(ts // 8):
        hg = a_s[8 * g:8 * g + 8, :] * hc + b_s[8 * g:8 * g + 8, :]
        b_s[8 * g:8 * g + 8, :] = hg
        hc = jnp.broadcast_to(hg[7:8, :], (8, cb))
    hc_ref[...] = hc

    o_ref[...] = (b_s[...] * _gelu_tanh(y_ref[...].astype(F32))).astype(BF16)


def _lru(proj3, w_conv, b_conv, w_rg2, b_rg, w_ig2, b_ig, lam, batch, seq):
    ts, cb = TS_LRU, CB_LRU
    ncb = LRU_WIDTH // cb
    x_off = 0
    y_off = x_off + ncb
    return pl.pallas_call(
        _lru_kernel,
        grid=(batch, ncb, seq // ts),
        in_specs=[
            pl.BlockSpec((None, ts, cb), lambda b, c, s: (b, s, x_off + c)),
            pl.BlockSpec((None, ts, cb), lambda b, c, s: (b, s, y_off + c)),
            pl.BlockSpec((LRU_CONV_WIDTH, cb), lambda b, c, s: (0, c)),
            pl.BlockSpec((1, cb), lambda b, c, s: (0, c)),
            pl.BlockSpec((None, cb, cb), lambda b, c, s: (c, 0, 0)),
            pl.BlockSpec((1, cb), lambda b, c, s: (0, c)),
            pl.BlockSpec((None, cb, cb), lambda b, c, s: (c, 0, 0)),
            pl.BlockSpec((1, cb), lambda b, c, s: (0, c)),
            pl.BlockSpec((1, cb), lambda b, c, s: (0, c)),
        ],
        out_specs=pl.BlockSpec((None, ts, cb), lambda b, c, s: (b, s, c)),
        out_shape=jax.ShapeDtypeStruct((batch, seq, LRU_WIDTH), BF16),
        scratch_shapes=[
            pltpu.VMEM((8, cb), F32),
            pltpu.VMEM((ts, cb), F32),
            pltpu.VMEM((ts, cb), F32),
            pltpu.VMEM((8, cb), F32),
        ],
        compiler_params=_params("arbitrary", "arbitrary", "arbitrary"),
        name="lru",
    )(proj3, proj3, w_conv, b_conv, w_rg2, b_rg, w_ig2, b_ig, lam)


def _merge_kernel(attn_ref, lru_ref, gm_ref, gl_ref, x_ref, wom_ref, wol_ref, wout_ref, g_ref,
                  o_ref):
    o_mla = jnp.dot(attn_ref[...], wom_ref[...], preferred_element_type=F32)
    o_lru = jnp.dot(lru_ref[...], wol_ref[...], preferred_element_type=F32)
    merged = (jax.nn.sigmoid(gm_ref[...].astype(F32)) * o_mla
              + jax.nn.sigmoid(gl_ref[...].astype(F32)) * o_lru)
    y = jnp.dot(merged.astype(BF16), wout_ref[...], preferred_element_type=F32)
    o_ref[...] = x_ref[...] + _rms(y, g_ref[...])


def _merge(attn2, lru2, proj2, x2d, w_om, w_ol, w_out, g_post):
    t = x2d.shape[0]
    tm = TM_MERGE
    gm_off = 2 * LRU_WIDTH // D_MODEL
    row = lambda i: (i, 0)
    const = lambda i: (0, 0)
    return pl.pallas_call(
        _merge_kernel,
        grid=(t // tm,),
        in_specs=[
            pl.BlockSpec((tm, D_MODEL), row),
            pl.BlockSpec((tm, D_MODEL), row),
            pl.BlockSpec((tm, D_MODEL), lambda i: (i, gm_off)),
            pl.BlockSpec((tm, D_MODEL), lambda i: (i, gm_off + 1)),
            pl.BlockSpec((tm, D_MODEL), row),
            _resident((D_MODEL, D_MODEL), const),
            _resident((D_MODEL, D_MODEL), const),
            _resident((D_MODEL, D_MODEL), const),
            pl.BlockSpec((1, D_MODEL), const),
        ],
        out_specs=pl.BlockSpec((tm, D_MODEL), row),
        out_shape=jax.ShapeDtypeStruct((t, D_MODEL), F32),
        compiler_params=_params("arbitrary"),
        name="merge",
    )(attn2, lru2, proj2, proj2, x2d, w_om, w_ol, w_out, g_post)


def _memkv_kernel(mem_ref, g_ref, wckt_ref, wcv_ref, ckt_ref, cv_ref):
    mn = _rms(mem_ref[...], g_ref[...]).astype(BF16)
    ckt_ref[...] = lax.dot_general(wckt_ref[...], mn, (((1,), (1,)), ((), ())),
                                   preferred_element_type=F32).astype(BF16)
    cv_ref[...] = jnp.dot(mn, wcv_ref[...], preferred_element_type=F32).astype(BF16)


def _mem_kv(mem, g_mem, w_ckt, w_cv):
    b, m, _ = mem.shape
    xd = X_HEADS * X_HEAD_DIM
    return pl.pallas_call(
        _memkv_kernel,
        grid=(b,),
        in_specs=[
            pl.BlockSpec((None, m, D_MODEL), lambda i: (i, 0, 0)),
            pl.BlockSpec((1, D_MODEL), lambda i: (0, 0)),
            pl.BlockSpec((xd, D_MODEL), lambda i: (0, 0)),
            pl.BlockSpec((D_MODEL, xd), lambda i: (0, 0)),
        ],
        out_specs=[
            pl.BlockSpec((None, xd, m), lambda i: (i, 0, 0)),
            pl.BlockSpec((None, m, xd), lambda i: (i, 0, 0)),
        ],
        out_shape=[
            jax.ShapeDtypeStruct((b, xd, m), BF16),
            jax.ShapeDtypeStruct((b, m, xd), BF16),
        ],
        compiler_params=_params("arbitrary"),
        name="mem_kv",
    )(mem, g_mem, w_ckt, w_cv)


def _cross_kernel(x_ref, gpre_ref, wcq_ref, ckt_ref, cv_ref, wco_ref, gpost_ref, o_ref):
    x = x_ref[...]
    h = _rms(x, gpre_ref[...]).astype(BF16)
    cq = jnp.dot(h, wcq_ref[...], preferred_element_type=F32).astype(BF16)
    outs = []
    for hh in range(X_HEADS):
        sl = slice(hh * X_HEAD_DIM, (hh + 1) * X_HEAD_DIM)
        s = jnp.dot(cq[:, sl], ckt_ref[sl, :], preferred_element_type=F32)
        p = jnp.exp2(s - jnp.max(s, axis=-1, keepdims=True))
        l = jnp.sum(p, axis=-1, keepdims=True)
        pv = jnp.dot(p.astype(BF16), cv_ref[:, sl], preferred_element_type=F32)
        outs.append((pv * (1.0 / l)).astype(BF16))
    co = jnp.concatenate(outs, axis=-1)
    y = jnp.dot(co, wco_ref[...], preferred_element_type=F32)
    o_ref[...] = x + _rms(y, gpost_ref[...])


def _cross(x2d, g_pre, w_cq, ckt, cv, w_co, g_post, seq):
    t = x2d.shape[0]
    tm = TM_CROSS
    xd = X_HEADS * X_HEAD_DIM
    m = cv.shape[1]
    per_b = seq // tm
    return pl.pallas_call(
        _cross_kernel,
        grid=(t // tm,),
        in_specs=[
            pl.BlockSpec((tm, D_MODEL), lambda i: (i, 0)),
            pl.BlockSpec((1, D_MODEL), lambda i: (0, 0)),
            pl.BlockSpec((D_MODEL, xd), lambda i: (0, 0)),
            pl.BlockSpec((None, xd, m), lambda i: (i // per_b, 0, 0)),
            pl.BlockSpec((None, m, xd), lambda i: (i // per_b, 0, 0)),
            pl.BlockSpec((xd, D_MODEL), lambda i: (0, 0)),
            pl.BlockSpec((1, D_MODEL), lambda i: (0, 0)),
        ],
        out_specs=pl.BlockSpec((tm, D_MODEL), lambda i: (i, 0)),
        out_shape=jax.ShapeDtypeStruct((t, D_MODEL), F32),
        compiler_params=_params("arbitrary"),
        name="cross",
    )(x2d, g_pre, w_cq, ckt, cv, w_co, g_post)


def _up_kernel(x_ref, halo_ref, g_ref, wg_ref, wv_ref, cwg_ref, cwv_ref, cbg_ref, cbv_ref,
               o_ref, h_ref, ag_ref, av_ref, *, tiles_per_seq):
    i = pl.program_id(0)
    j = pl.program_id(1)
    tm, halo = TM_UP, HALO_UP

    @pl.when(j == 0)
    def _():
        g = g_ref[...]
        keep = jnp.where(i % tiles_per_seq == 0, 0.0, 1.0)
        h_ref[0:halo, :] = (_rms(halo_ref[...], g) * keep).astype(BF16)
        h_ref[halo:halo + tm, :] = _rms(x_ref[...], g).astype(BF16)

    h = h_ref[...]
    ag_ref[...] = jnp.dot(h, wg_ref[...], preferred_element_type=F32)
    av_ref[...] = jnp.dot(h, wv_ref[...], preferred_element_type=F32)

    def conv(a_ref, w_ref, b_ref):
        acc = b_ref[...]
        for k in range(FFN_CONV_WIDTH):
            acc = acc + a_ref[pl.ds(halo - (FFN_CONV_WIDTH - 1) + k, tm), :] * w_ref[k:k + 1, :]
        return acc

    gate = conv(ag_ref, cwg_ref, cbg_ref)
    val = conv(av_ref, cwv_ref, cbv_ref)
    o_ref[...] = (_gelu_tanh(gate) * val).astype(BF16)


def _ffn_up(x2d, g_pre, w_up, w_fconv, b_fconv, seq):
    t = x2d.shape[0]
    tm, tn, halo = TM_UP, TN_UP, HALO_UP
    nj = D_FF // tn
    return pl.pallas_call(
        functools.partial(_up_kernel, tiles_per_seq=seq // tm),
        grid=(t // tm, nj),
        in_specs=[
            pl.BlockSpec((tm, D_MODEL), lambda i, j: (i, 0)),
            pl.BlockSpec((halo, D_MODEL), lambda i, j: (jnp.maximum(i * (tm // halo) - 1, 0), 0)),
            pl.BlockSpec((1, D_MODEL), lambda i, j: (0, 0)),
            pl.BlockSpec((D_MODEL, tn), lambda i, j: (0, j)),
            pl.BlockSpec((D_MODEL, tn), lambda i, j: (0, j + nj)),
            pl.BlockSpec((FFN_CONV_WIDTH, tn), lambda i, j: (0, j)),
            pl.BlockSpec((FFN_CONV_WIDTH, tn), lambda i, j: (0, j + nj)),
            pl.BlockSpec((1, tn), lambda i, j: (0, j)),
            pl.BlockSpec((1, tn), lambda i, j: (0, j + nj)),
        ],
        out_specs=pl.BlockSpec((tm, tn), lambda i, j: (i, j)),
        out_shape=jax.ShapeDtypeStruct((t, D_FF), BF16),
        scratch_shapes=[
            pltpu.VMEM((tm + halo, D_MODEL), BF16),
            pltpu.VMEM((tm + halo, tn), F32),
            pltpu.VMEM((tm + halo, tn), F32),
        ],
        compiler_params=_params("arbitrary", "arbitrary"),
        name="ffn_up",
    )(x2d, x2d, g_pre, w_up, w_up, w_fconv, w_fconv, b_fconv, b_fconv)


def _down_kernel(a_ref, x_ref, w_ref, g_ref, o_ref):
    y = jnp.dot(a_ref[...], w_ref[...], preferred_element_type=F32)
    o_ref[...] = x_ref[...] + _rms(y, g_ref[...])


def _ffn_down(act, x2d, w_down, g_post):
    t = x2d.shape[0]
    tm = TM_DOWN
    return pl.pallas_call(
        _down_kernel,
        grid=(t // tm,),
        in_specs=[
            pl.BlockSpec((tm, D_FF), lambda i: (i, 0)),
            pl.BlockSpec((tm, D_MODEL), lambda i: (i, 0)),
            _resident((D_FF, D_MODEL), lambda i: (0, 0)),
            pl.BlockSpec((1, D_MODEL), lambda i: (0, 0)),
        ],
        out_specs=pl.BlockSpec((tm, D_MODEL), lambda i: (i, 0)),
        out_shape=jax.ShapeDtypeStruct((t, D_MODEL), F32),
        compiler_params=_params("arbitrary"),
        name="ffn_down",
    )(act, x2d, w_down, g_post)


def _rope_angles(positions):
    assert QK_ROPE_DIM * 2 == LANES
    inv_freq = ROPE_THETA ** (-jnp.arange(0, QK_ROPE_DIM, 2, dtype=F32) / QK_ROPE_DIM)
    ang = positions.astype(F32).reshape(-1, 1) * inv_freq
    return jnp.cos(ang), jnp.sin(ang)


def _block_diag_pairs(w):
    nb, d, _ = w.shape
    w = w.reshape(nb // 2, 2, d, d)
    z = jnp.zeros_like(w[:, 0])
    top = jnp.concatenate([w[:, 0], z], axis=-1)
    bot = jnp.concatenate([z, w[:, 1]], axis=-1)
    return jnp.concatenate([top, bot], axis=-2)


def kernel(x, mem, positions, g_pre_mix, g_post_mix, w_in, g_ckv, w_ukv, w_o_mla, w_conv_lru, b_conv_lru, w_rg, b_rg, w_ig, b_ig, lru_lambda, w_o_lru, w_out, g_pre_x, g_post_x, g_mem, w_cq, w_ck, w_cv, w_co, g_pre_ffn, g_post_ffn, w_up, w_fconv, b_fconv, w_down):
    batch, seq, d = x.shape
    assert d == D_MODEL and g_pre_mix.shape[0] == 1
    t = batch * seq
    x2d = x.reshape(t, d)
    l = 0

    q_scale = (QK_HEAD_DIM ** -0.5) * LOG2E
    wi_t = w_in[l].T
    wq_t = (wi_t[:Q_COLS] * q_scale).astype(BF16)
    kv_end = Q_COLS + KV_LORA_RANK + QK_ROPE_DIM
    wr_t = wi_t[kv_end:].astype(BF16)
    wb_t = jnp.pad(wi_t[Q_COLS:kv_end], ((0, LANES - QK_ROPE_DIM), (0, 0))).astype(BF16)
    wukv = w_ukv[l].reshape(KV_LORA_RANK, MLA_HEADS, QK_NOPE_DIM + V_HEAD_DIM)
    w_uk = wukv[:, :, :QK_NOPE_DIM].reshape(KV_LORA_RANK, -1).astype(BF16)
    w_uvt = wukv[:, :, QK_NOPE_DIM:].reshape(KV_LORA_RANK, -1).T.astype(BF16)
    row = lambda v: v.reshape(1, -1)
    cos, sin = _rope_angles(positions)

    q_t, proj = _proj_in(x2d, row(g_pre_mix[l]), wq_t, wr_t, cos.T, sin.T, batch, seq)
    kn, kr, vt = _kv_path(x2d, row(g_pre_mix[l]), wb_t, row(g_ckv[l]), w_uk, w_uvt, cos, sin,
                          batch, seq)
    proj3 = proj.reshape(batch, seq, -1)
    attn = _mla_attn(q_t, kn.reshape(batch, seq, -1), kr.reshape(batch, seq, LANES),
                     vt.reshape(batch, seq // TM_KV, MLA_HEADS * V_HEAD_DIM, TM_KV), batch, seq)
    lru = _lru(proj3, w_conv_lru[l], row(b_conv_lru[l]),
               _block_diag_pairs(w_rg[l]).astype(BF16), row(b_rg[l]),
               _block_diag_pairs(w_ig[l]).astype(BF16), row(b_ig[l]),
               row(lru_lambda[l]), batch, seq)
    x1 = _merge(attn.reshape(t, -1), lru.reshape(t, -1), proj, x2d,
                w_o_mla[l].astype(BF16), w_o_lru[l].astype(BF16), w_out[l].astype(BF16),
                row(g_post_mix[l]))

    c_scale = (X_HEAD_DIM ** -0.5) * LOG2E
    ckt, cv = _mem_kv(mem, row(g_mem[l]), w_ck[l].T.astype(BF16), w_cv[l].astype(BF16))
    x2 = _cross(x1, row(g_pre_x[l]), (w_cq[l] * c_scale).astype(BF16), ckt, cv,
                w_co[l].astype(BF16), row(g_post_x[l]), seq)

    act = _ffn_up(x2, row(g_pre_ffn[l]), w_up[l].astype(BF16), w_fconv[l], row(b_fconv[l]), seq)
    x3 = _ffn_down(act, x2, w_down[l].astype(BF16), row(g_post_ffn[l]))
    return x3.reshape(batch, seq, d)
```

```python
import functools
import math

import jax
import jax.numpy as jnp
from jax import lax
from jax.experimental import pallas as pl
from jax.experimental.pallas import tpu as pltpu

F32 = jnp.float32
BF16 = jnp.bfloat16

D_MODEL = 2048
MLA_HEADS = 16
QK_NOPE_DIM = 128
QK_ROPE_DIM = 64
V_HEAD_DIM = 128
KV_LORA_RANK = 512
QK_HEAD_DIM = QK_NOPE_DIM + QK_ROPE_DIM
ROPE_THETA = 10000.0
LRU_WIDTH = D_MODEL
LRU_BLOCKS = 16
LRU_BLOCK_DIM = LRU_WIDTH // LRU_BLOCKS
LRU_CONV_WIDTH = 4
LRU_C = 8.0
X_HEADS = 4
X_HEAD_DIM = 128
D_FF = 5632
FFN_CONV_WIDTH = 3
EPS = 1e-6

LOG2E = math.log2(math.e)
LANES = 128
BF16_ROWS = 16
Q_PAD_DIM = 2 * LANES
V_AUG_DIM = V_HEAD_DIM + BF16_ROWS
Q_COLS = MLA_HEADS * QK_HEAD_DIM
VMEM_LIMIT = 56 * 1024 * 1024
NT_DIMS = (((1,), (1,)), ((), ()))

TM_PROJ, TN_PROJ = 1024, 1024
TN_Q = 4 * QK_HEAD_DIM
TM_KV = 512
TQ, TK = 1024, 1024
assert TQ == TK and TK % TM_KV == 0
Q_CHUNKS = 2
PROBE_KEYS = BF16_ROWS
SCORE_JUMP_LIMIT = 64.0
TS_LRU, CB_LRU = 512, 256
TM_MERGE = 256
TM_CROSS = 512
TM_UP, TN_UP, HALO_UP = 1024, 512, 16
TM_DOWN = 256


def _params(*sem):
    return pltpu.CompilerParams(dimension_semantics=sem, vmem_limit_bytes=VMEM_LIMIT)


def _resident(shape, index_map):
    return pl.BlockSpec(shape, index_map, pipeline_mode=pl.Buffered(1))


def _rms(x, g):
    return x * lax.rsqrt(jnp.mean(x * x, axis=-1, keepdims=True) + EPS) * g


def _gelu_tanh(x):
    c = math.sqrt(2.0 / math.pi)
    return 0.5 * x * (1.0 + jnp.tanh(c * (x + 0.044715 * (x * x * x))))


def _rope_tables(cos_ref, sin_ref):
    c = cos_ref[...]
    s = sin_ref[...]
    z = jnp.zeros_like(c)
    return jnp.concatenate([c, c, z, z, -s, z, z, z, z, s, z, z], axis=-1)


def _rope_slab(slab, trig):
    c = trig[:, 0:LANES]
    s1 = trig[:, LANES:2 * LANES]
    s2 = trig[:, 2 * LANES:3 * LANES]
    return slab * c + pltpu.roll(slab, 96, 1) * s1 + pltpu.roll(slab, 32, 1) * s2


def _proj_kernel(x_ref, g_ref, wq_ref, wr_ref, cos_ref, sin_ref, qt_ref, o_ref, h_ref, *,
                 n_q_tiles):
    j = pl.program_id(1)

    @pl.when(j == 0)
    def _():
        h_ref[...] = _rms(x_ref[...], g_ref[...]).astype(BF16)

    @pl.when(j >= n_q_tiles)
    def _():
        o_ref[...] = lax.dot_general(h_ref[...], wr_ref[...], NT_DIMS,
                                     preferred_element_type=F32).astype(BF16)

    @pl.when(j < n_q_tiles)
    def _():
        acc = lax.dot_general(wq_ref[...], h_ref[...], NT_DIMS, preferred_element_type=F32)
        cos_t = cos_ref[...]
        sin_t = sin_ref[...]
        half = QK_ROPE_DIM // 2
        for hh in range(TN_Q // QK_HEAD_DIM):
            r0 = hh * QK_HEAD_DIM
            r1 = r0 + QK_NOPE_DIM
            qt_ref[r0:r1, :] = acc[r0:r1, :].astype(BF16)
            x1 = acc[r1:r1 + half, :]
            x2 = acc[r1 + half:r1 + 2 * half, :]
            qt_ref[r1:r1 + half, :] = (x1 * cos_t - x2 * sin_t).astype(BF16)
            qt_ref[r1 + half:r1 + 2 * half, :] = (x2 * cos_t + x1 * sin_t).astype(BF16)


def _proj_in(x2d, g, wq_t, wr_t, cos_t, sin_t, batch, seq):
    t = x2d.shape[0]
    nq = wq_t.shape[0] // TN_Q
    nr = wr_t.shape[0] // TN_PROJ
    per_b = seq // TM_PROJ
    half = QK_ROPE_DIM // 2
    return pl.pallas_call(
        functools.partial(_proj_kernel, n_q_tiles=nq),
        grid=(t // TM_PROJ, nq + nr),
        in_specs=[
            pl.BlockSpec((TM_PROJ, D_MODEL), lambda i, j: (i, 0)),
            pl.BlockSpec((1, D_MODEL), lambda i, j: (0, 0)),
            pl.BlockSpec((TN_Q, D_MODEL), lambda i, j: (jnp.minimum(j, nq - 1), 0)),
            pl.BlockSpec((TN_PROJ, D_MODEL), lambda i, j: (jnp.maximum(j - nq, 0), 0)),
            pl.BlockSpec((half, TM_PROJ), lambda i, j: (0, i)),
            pl.BlockSpec((half, TM_PROJ), lambda i, j: (0, i)),
        ],
        out_specs=[
            pl.BlockSpec((None, TN_Q, TM_PROJ),
                         lambda i, j: (i // per_b, jnp.minimum(j, nq - 1), i % per_b)),
            pl.BlockSpec((TM_PROJ, TN_PROJ), lambda i, j: (i, jnp.maximum(j - nq, 0))),
        ],
        out_shape=[
            jax.ShapeDtypeStruct((batch, wq_t.shape[0], seq), BF16),
            jax.ShapeDtypeStruct((t, wr_t.shape[0]), BF16),
        ],
        scratch_shapes=[pltpu.VMEM((TM_PROJ, D_MODEL), BF16)],
        compiler_params=_params("arbitrary", "arbitrary"),
        name="proj_in",
    )(x2d, g, wq_t, wr_t, cos_t, sin_t)


def _kv_kernel(x_ref, g_ref, wb_ref, gc_ref, wuk_ref, wuvt_ref, cos_ref, sin_ref,
               kn_ref, kr_ref, vt_ref):
    h = _rms(x_ref[...], g_ref[...]).astype(BF16)
    c = lax.dot_general(h, wb_ref[...], NT_DIMS, preferred_element_type=F32)
    n = _rms(c[:, :KV_LORA_RANK], gc_ref[...]).astype(BF16)
    kn_ref[...] = jnp.dot(n, wuk_ref[...], preferred_element_type=F32).astype(BF16)
    vt_ref[...] = lax.dot_general(wuvt_ref[...], n, NT_DIMS,
                                  preferred_element_type=F32).astype(BF16)
    kr_ref[...] = _rope_slab(c[:, KV_LORA_RANK:], _rope_tables(cos_ref, sin_ref)).astype(BF16)


def _kv_path(x2d, g, w_b, g_ckv, w_uk, w_uvt, cos, sin, batch, seq):
    t = x2d.shape[0]
    nk = seq // TM_KV
    hv = MLA_HEADS * V_HEAD_DIM
    return pl.pallas_call(
        _kv_kernel,
        grid=(t // TM_KV,),
        in_specs=[
            pl.BlockSpec((TM_KV, D_MODEL), lambda i: (i, 0)),
            pl.BlockSpec((1, D_MODEL), lambda i: (0, 0)),
            _resident(w_b.shape, lambda i: (0, 0)),
            pl.BlockSpec((1, KV_LORA_RANK), lambda i: (0, 0)),
            _resident(w_uk.shape, lambda i: (0, 0)),
            _resident(w_uvt.shape, lambda i: (0, 0)),
            pl.BlockSpec((TM_KV, QK_ROPE_DIM // 2), lambda i: (i, 0)),
            pl.BlockSpec((TM_KV, QK_ROPE_DIM // 2), lambda i: (i, 0)),
        ],
        out_specs=[
            pl.BlockSpec((TM_KV, hv), lambda i: (i, 0)),
            pl.BlockSpec((TM_KV, LANES), lambda i: (i, 0)),
            pl.BlockSpec((None, hv, TM_KV), lambda i: (i, 0, 0)),
        ],
        out_shape=[
            jax.ShapeDtypeStruct((t, hv), BF16),
            jax.ShapeDtypeStruct((t, LANES), BF16),
            jax.ShapeDtypeStruct((batch * nk, hv, TM_KV), BF16),
        ],
        compiler_params=_params("arbitrary"),
        name="kv_path",
    )(x2d, g, w_b, g_ckv, w_uk, w_uvt, cos, sin)


def _attn_kernel(q_ref, kn_ref, kr_ref, vt_ref, o_ref,
                 kcat_ref, vaug_ref, qt_ref, p0, p1, a0, a1, acc_ref, r_ref, rprev_ref, jump_ref):
    i = pl.program_id(2)
    n = i + 1

    @pl.when(i == 0)
    def _():
        kcat_ref[:, 0:LANES] = kn_ref[...]
        kcat_ref[:, LANES:Q_PAD_DIM] = kr_ref[...]
        sub = TK // TM_KV
        for j in range(vaug_ref.shape[0]):
            for c in range(sub):
                vaug_ref[j, 0:V_HEAD_DIM, c * TM_KV:(c + 1) * TM_KV] = vt_ref[j * sub + c]
        vaug_ref[:, V_HEAD_DIM:V_AUG_DIM, :] = jnp.ones(
            (vaug_ref.shape[0], V_AUG_DIM - V_HEAD_DIM, TK), BF16)
        qt_ref[QK_HEAD_DIM:Q_PAD_DIM, :] = jnp.zeros((Q_PAD_DIM - QK_HEAD_DIM, TQ), BF16)

    qt_ref[0:QK_HEAD_DIM, :] = q_ref[...]
    acc_ref[...] = jnp.zeros(acc_ref.shape, F32)

    s_probe = jnp.dot(kcat_ref[0:PROBE_KEYS, :], qt_ref[...], preferred_element_type=F32)
    key = lax.broadcasted_iota(jnp.int32, s_probe.shape, 0)
    qry = lax.broadcasted_iota(jnp.int32, s_probe.shape, 1) + i * TQ
    r0 = jnp.max(jnp.where(key <= qry, s_probe, -jnp.inf), axis=0, keepdims=True)
    r_ref[...] = r0
    rprev_ref[...] = r0
    jump_ref[...] = jnp.zeros(jump_ref.shape, F32)

    def qs(b, p_ref, a_ref, diagonal, cols):
        nkeys = cols.stop if diagonal else TK
        k0 = pl.multiple_of(b * TK, TK)
        s = jnp.dot(kcat_ref[pl.ds(k0, nkeys), :], qt_ref[:, cols],
                    preferred_element_type=F32)
        if diagonal:
            key = lax.broadcasted_iota(jnp.int32, s.shape, 0)
            qry = lax.broadcasted_iota(jnp.int32, s.shape, 1) + cols.start
            s = jnp.where(key <= qry, s, -jnp.inf)
        r = r_ref[:, cols]
        p_ref[0:nkeys, cols] = jnp.exp2(s - r).astype(BF16)
        a_ref[:, cols] = jnp.exp2(rprev_ref[:, cols] - r)
        rprev_ref[:, cols] = r
        block_max = jnp.max(s, axis=0, keepdims=True)
        r_ref[:, cols] = jnp.maximum(r, block_max)
        jump_ref[:, cols] = jnp.maximum(jump_ref[:, cols], block_max - r)

    def pv(b, p_ref, a_ref, diagonal, cols):
        nkeys = cols.stop if diagonal else TK
        acc_ref[:, cols] = acc_ref[:, cols] * a_ref[:, cols] + jnp.dot(
            vaug_ref[b, :, 0:nkeys], p_ref[0:nkeys, cols],
            preferred_element_type=F32)

    bufs = ((p0, a0), (p1, a1))

    def step(t, parity, do_qs=False, diagonal=False, do_pv=False, pv_diagonal=False):
        p_c, a_c = bufs[parity]
        p_p, a_p = bufs[1 - parity]
        for c in range(Q_CHUNKS):
            cols = slice(c * (TQ // Q_CHUNKS), (c + 1) * (TQ // Q_CHUNKS))
            if do_qs:
                qs(t, p_c, a_c, diagonal, cols)
            if do_pv:
                pv(t - 1, p_p, a_p, pv_diagonal, cols)

    def by_parity(t, cond, **stages):
        for parity in (0, 1):
            @pl.when(jnp.logical_and(cond, t % 2 == parity))
            def _():
                step(t, parity, **stages)

    @pl.when(n == 1)
    def _():
        step(0, 0, do_qs=True, diagonal=True)

    @pl.when(n >= 2)
    def _():
        step(0, 0, do_qs=True)

    def steady(t, carry):
        by_parity(t, True, do_qs=True, do_pv=True)
        return carry

    lax.fori_loop(1, n - 1, steady, 0)
    by_parity(n - 1, n >= 2, do_qs=True, diagonal=True, do_pv=True)
    by_parity(n, True, do_pv=True, pv_diagonal=True)

    @pl.when(jnp.max(jump_ref[...]) > SCORE_JUMP_LIMIT)
    def _():
        acc_ref[...] = jnp.zeros(acc_ref.shape, F32)

        def block(b, m_old):
            k0 = pl.multiple_of(b * TK, TK)
            s = jnp.dot(kcat_ref[pl.ds(k0, TK), :], qt_ref[...], preferred_element_type=F32)
            key = lax.broadcasted_iota(jnp.int32, s.shape, 0) + b * TK
            qry = lax.broadcasted_iota(jnp.int32, s.shape, 1) + i * TQ
            s = jnp.where(key <= qry, s, -jnp.inf)
            m_new = jnp.maximum(m_old, jnp.max(s, axis=0, keepdims=True))
            p = jnp.exp2(s - m_new).astype(BF16)
            acc_ref[...] = acc_ref[...] * jnp.exp2(m_old - m_new) + jnp.dot(
                vaug_ref[b], p, preferred_element_type=F32)
            return m_new

        lax.fori_loop(0, n, block, jnp.full((1, TQ), -jnp.inf, F32))

    out = acc_ref[0:V_HEAD_DIM, :] * (1.0 / acc_ref[V_HEAD_DIM:V_HEAD_DIM + 1, :])
    o_ref[...] = out.T.astype(BF16)


def _mla_attn(q_t, kn3, kr3, vt4, batch, seq):
    nk = seq // TK
    hv = MLA_HEADS * V_HEAD_DIM
    return pl.pallas_call(
        _attn_kernel,
        grid=(batch, MLA_HEADS, seq // TQ),
        in_specs=[
            pl.BlockSpec((None, QK_HEAD_DIM, TQ), lambda b, h, i: (b, h, i)),
            pl.BlockSpec((None, seq, LANES), lambda b, h, i: (b, 0, h)),
            pl.BlockSpec((None, seq, LANES), lambda b, h, i: (b, 0, 0)),
            pl.BlockSpec((None, seq // TM_KV, V_HEAD_DIM, TM_KV), lambda b, h, i: (b, 0, h, 0)),
        ],
        out_specs=pl.BlockSpec((None, TQ, V_HEAD_DIM), lambda b, h, i: (b, i, h)),
        out_shape=jax.ShapeDtypeStruct((batch, seq, hv), BF16),
        scratch_shapes=[
            pltpu.VMEM((seq, Q_PAD_DIM), BF16),
            pltpu.VMEM((nk, V_AUG_DIM, TK), BF16),
            pltpu.VMEM((Q_PAD_DIM, TQ), BF16),
            pltpu.VMEM((TK, TQ), BF16),
            pltpu.VMEM((TK, TQ), BF16),
            pltpu.VMEM((1, TQ), F32),
            pltpu.VMEM((1, TQ), F32),
            pltpu.VMEM((V_AUG_DIM, TQ), F32),
            pltpu.VMEM((1, TQ), F32),
            pltpu.VMEM((1, TQ), F32),
            pltpu.VMEM((1, TQ), F32),
        ],
        compiler_params=_params("arbitrary", "arbitrary", "arbitrary"),
        name="mla_attn",
    )(q_t, kn3, kr3, vt4)


def _lru_kernel(x_ref, y_ref, wc_ref, bc_ref, wrg_ref, brg_ref, wig_ref, big_ref, lam_ref,
                o_ref, xprev, a_s, b_s, hc_ref):
    si = pl.program_id(2)
    ts, cb = TS_LRU, CB_LRU

    @pl.when(si == 0)
    def _():
        xprev[...] = jnp.zeros((8, cb), F32)
        hc_ref[...] = jnp.zeros((8, cb), F32)

    x = x_ref[...].astype(F32)
    prev = xprev[...]
    row8 = lax.broadcasted_iota(jnp.int32, (8, cb), 0)
    last = LRU_CONV_WIDTH - 1
    xc = bc_ref[...] + x * wc_ref[last:last + 1, :]
    for k in range(1, LRU_CONV_WIDTH):
        xs = pltpu.roll(x, k, 0)
        head = jnp.where(row8 < k, pltpu.roll(prev, k, 0), xs[0:8, :])
        xs = jnp.concatenate([head, xs[8:, :]], axis=0)
        xc = xc + xs * wc_ref[last - k:last - k + 1, :]
    xprev[...] = x[ts - 8:, :]

    xcb = xc.astype(BF16)
    r = jax.nn.sigmoid(jnp.dot(xcb, wrg_ref[...], preferred_element_type=F32) + brg_ref[...])
    ig = jax.nn.sigmoid(jnp.dot(xcb, wig_ref[...], preferred_element_type=F32) + big_ref[...])
    z = -lam_ref[...]
    softplus = jnp.maximum(z, 0.0) + jnp.log1p(jnp.exp(-jnp.abs(z)))
    a = jnp.exp((-LRU_C * softplus) * r)
    v = 1.0 - a * a
    u = jnp.where(v > 0.0, v * lax.rsqrt(v), 0.0) * (ig * xc)

    ng = ts // 8
    a = a.reshape(ng, 8, cb)
    u = u.reshape(ng, 8, cb)
    row = lax.broadcasted_iota(jnp.int32, (ng, 8, cb), 1)
    for d in (1, 2, 4):
        keep = row >= d
        a_sh = pltpu.roll(a, d, 1)
        u_sh = pltpu.roll(u, d, 1)
        u = jnp.where(keep, a, 0.0) * u_sh + u
        a = a * jnp.where(keep, a_sh, 1.0)
    a_s[...] = a.reshape(ts, cb)
    b_s[...] = u.reshape(ts, cb)

    hc = hc_ref[...]
    for g in range(ts // 8):
        hg = a_s[8 * g:8 * g + 8, :] * hc + b_s[8 * g:8 * g + 8, :]
        b_s[8 * g:8 * g + 8, :] = hg
        hc = jnp.broadcast_to(hg[7:8, :], (8, cb))
    hc_ref[...] = hc

    o_ref[...] = (b_s[...] * _gelu_tanh(y_ref[...].astype(F32))).astype(BF16)


def _lru(proj3, w_conv, b_conv, w_rg2, b_rg, w_ig2, b_ig, lam, batch, seq):
    ts, cb = TS_LRU, CB_LRU
    ncb = LRU_WIDTH // cb
    x_off = 0
    y_off = x_off + ncb
    return pl.pallas_call(
        _lru_kernel,
        grid=(batch, ncb, seq // ts),
        in_specs=[
            pl.BlockSpec((None, ts, cb), lambda b, c, s: (b, s, x_off + c)),
            pl.BlockSpec((None, ts, cb), lambda b, c, s: (b, s, y_off + c)),
            pl.BlockSpec((LRU_CONV_WIDTH, cb), lambda b, c, s: (0, c)),
            pl.BlockSpec((1, cb), lambda b, c, s: (0, c)),
            pl.BlockSpec((None, cb, cb), lambda b, c, s: (c, 0, 0)),
            pl.BlockSpec((1, cb), lambda b, c, s: (0, c)),
            pl.BlockSpec((None, cb, cb), lambda b, c, s: (c, 0, 0)),
            pl.BlockSpec((1, cb), lambda b, c, s: (0, c)),
            pl.BlockSpec((1, cb), lambda b, c, s: (0, c)),
        ],
        out_specs=pl.BlockSpec((None, ts, cb), lambda b, c, s: (b, s, c)),
        out_shape=jax.ShapeDtypeStruct((batch, seq, LRU_WIDTH), BF16),
        scratch_shapes=[
            pltpu.VMEM((8, cb), F32),
            pltpu.VMEM((ts, cb), F32),
            pltpu.VMEM((ts, cb), F32),
            pltpu.VMEM((8, cb), F32),
        ],
        compiler_params=_params("arbitrary", "arbitrary", "arbitrary"),
        name="lru",
    )(proj3, proj3, w_conv, b_conv, w_rg2, b_rg, w_ig2, b_ig, lam)


def _merge_kernel(attn_ref, lru_ref, gm_ref, gl_ref, x_ref, wom_ref, wol_ref, wout_ref, g_ref,
                  o_ref):
    o_mla = jnp.dot(attn_ref[...], wom_ref[...], preferred_element_type=F32)
    o_lru = jnp.dot(lru_ref[...], wol_ref[...], preferred_element_type=F32)
    merged = (jax.nn.sigmoid(gm_ref[...].astype(F32)) * o_mla
              + jax.nn.sigmoid(gl_ref[...].astype(F32)) * o_lru)
    y = jnp.dot(merged.astype(BF16), wout_ref[...], preferred_element_type=F32)
    o_ref[...] = x_ref[...] + _rms(y, g_ref[...])


def _merge(attn2, lru2, proj2, x2d, w_om, w_ol, w_out, g_post):
    t = x2d.shape[0]
    tm = TM_MERGE
    gm_off = 2 * LRU_WIDTH // D_MODEL
    row = lambda i: (i, 0)
    const = lambda i: (0, 0)
    return pl.pallas_call(
        _merge_kernel,
        grid=(t // tm,),
        in_specs=[
            pl.BlockSpec((tm, D_MODEL), row),
            pl.BlockSpec((tm, D_MODEL), row),
            pl.BlockSpec((tm, D_MODEL), lambda i: (i, gm_off)),
            pl.BlockSpec((tm, D_MODEL), lambda i: (i, gm_off + 1)),
            pl.BlockSpec((tm, D_MODEL), row),
            _resident((D_MODEL, D_MODEL), const),
            _resident((D_MODEL, D_MODEL), const),
            _resident((D_MODEL, D_MODEL), const),
            pl.BlockSpec((1, D_MODEL), const),
        ],
        out_specs=pl.BlockSpec((tm, D_MODEL), row),
        out_shape=jax.ShapeDtypeStruct((t, D_MODEL), F32),
        compiler_params=_params("arbitrary"),
        name="merge",
    )(attn2, lru2, proj2, proj2, x2d, w_om, w_ol, w_out, g_post)


def _memkv_kernel(mem_ref, g_ref, wckt_ref, wcv_ref, ckt_ref, cv_ref):
    mn = _rms(mem_ref[...], g_ref[...]).astype(BF16)
    ckt_ref[...] = lax.dot_general(wckt_ref[...], mn, (((1,), (1,)), ((), ())),
                                   preferred_element_type=F32).astype(BF16)
    cv_ref[...] = jnp.dot(mn, wcv_ref[...], preferred_element_type=F32).astype(BF16)


def _mem_kv(mem, g_mem, w_ckt, w_cv):
    b, m, _ = mem.shape
    xd = X_HEADS * X_HEAD_DIM
    return pl.pallas_call(
        _memkv_kernel,
        grid=(b,),
        in_specs=[
            pl.BlockSpec((None, m, D_MODEL), lambda i: (i, 0, 0)),
            pl.BlockSpec((1, D_MODEL), lambda i: (0, 0)),
            pl.BlockSpec((xd, D_MODEL), lambda i: (0, 0)),
            pl.BlockSpec((D_MODEL, xd), lambda i: (0, 0)),
        ],
        out_specs=[
            pl.BlockSpec((None, xd, m), lambda i: (i, 0, 0)),
            pl.BlockSpec((None, m, xd), lambda i: (i, 0, 0)),
        ],
        out_shape=[
            jax.ShapeDtypeStruct((b, xd, m), BF16),
            jax.ShapeDtypeStruct((b, m, xd), BF16),
        ],
        compiler_params=_params("arbitrary"),
        name="mem_kv",
    )(mem, g_mem, w_ckt, w_cv)


def _cross_kernel(x_ref, gpre_ref, wcq_ref, ckt_ref, cv_ref, wco_ref, gpost_ref, o_ref):
    x = x_ref[...]
    h = _rms(x, gpre_ref[...]).astype(BF16)
    cq = jnp.dot(h, wcq_ref[...], preferred_element_type=F32).astype(BF16)
    outs = []
    for hh in range(X_HEADS):
        sl = slice(hh * X_HEAD_DIM, (hh + 1) * X_HEAD_DIM)
        s = jnp.dot(cq[:, sl], ckt_ref[sl, :], preferred_element_type=F32)
        p = jnp.exp2(s - jnp.max(s, axis=-1, keepdims=True))
        l = jnp.sum(p, axis=-1, keepdims=True)
        pv = jnp.dot(p.astype(BF16), cv_ref[:, sl], preferred_element_type=F32)
        outs.append((pv * (1.0 / l)).astype(BF16))
    co = jnp.concatenate(outs, axis=-1)
    y = jnp.dot(co, wco_ref[...], preferred_element_type=F32)
    o_ref[...] = x + _rms(y, gpost_ref[...])


def _cross(x2d, g_pre, w_cq, ckt, cv, w_co, g_post, seq):
    t = x2d.shape[0]
    tm = TM_CROSS
    xd = X_HEADS * X_HEAD_DIM
    m = cv.shape[1]
    per_b = seq // tm
    return pl.pallas_call(
        _cross_kernel,
        grid=(t // tm,),
        in_specs=[
            pl.BlockSpec((tm, D_MODEL), lambda i: (i, 0)),
            pl.BlockSpec((1, D_MODEL), lambda i: (0, 0)),
            pl.BlockSpec((D_MODEL, xd), lambda i: (0, 0)),
            pl.BlockSpec((None, xd, m), lambda i: (i // per_b, 0, 0)),
            pl.BlockSpec((None, m, xd), lambda i: (i // per_b, 0, 0)),
            pl.BlockSpec((xd, D_MODEL), lambda i: (0, 0)),
            pl.BlockSpec((1, D_MODEL), lambda i: (0, 0)),
        ],
        out_specs=pl.BlockSpec((tm, D_MODEL), lambda i: (i, 0)),
        out_shape=jax.ShapeDtypeStruct((t, D_MODEL), F32),
        compiler_params=_params("arbitrary"),
        name="cross",
    )(x2d, g_pre, w_cq, ckt, cv, w_co, g_post)


def _up_kernel(x_ref, halo_ref, g_ref, wg_ref, wv_ref, cwg_ref, cwv_ref, cbg_ref, cbv_ref,
               o_ref, h_ref, *, tiles_per_seq):
    i = pl.program_id(0)
    j = pl.program_id(1)
    tm, halo = TM_UP, HALO_UP

    @pl.when(j == 0)
    def _():
        g = g_ref[...]
        keep = jnp.where(i % tiles_per_seq == 0, 0.0, 1.0)
        h_ref[0:halo, :] = (_rms(halo_ref[...], g) * keep).astype(BF16)
        h_ref[halo:halo + tm, :] = _rms(x_ref[...], g).astype(BF16)

    def conv(w_ref, cw_ref, cb_ref):
        a = jnp.dot(h_ref[...], w_ref[...], preferred_element_type=F32)
        y = a * cw_ref[0:1, :]
        for k in range(1, FFN_CONV_WIDTH):
            y = pltpu.roll(y, 1, 0) + a * cw_ref[k:k + 1, :]
        return y[halo:, :] + cb_ref[...]

    gate = conv(wg_ref, cwg_ref, cbg_ref)
    val = conv(wv_ref, cwv_ref, cbv_ref)
    o_ref[...] = (_gelu_tanh(gate) * val).astype(BF16)


def _ffn_up(x2d, g_pre, w_up, w_fconv, b_fconv, seq):
    t = x2d.shape[0]
    tm, tn, halo = TM_UP, TN_UP, HALO_UP
    nj = D_FF // tn
    return pl.pallas_call(
        functools.partial(_up_kernel, tiles_per_seq=seq // tm),
        grid=(t // tm, nj),
        in_specs=[
            pl.BlockSpec((tm, D_MODEL), lambda i, j: (i, 0)),
            pl.BlockSpec((halo, D_MODEL), lambda i, j: (jnp.maximum(i * (tm // halo) - 1, 0), 0)),
            pl.BlockSpec((1, D_MODEL), lambda i, j: (0, 0)),
            pl.BlockSpec((D_MODEL, tn), lambda i, j: (0, j)),
            pl.BlockSpec((D_MODEL, tn), lambda i, j: (0, j + nj)),
            pl.BlockSpec((FFN_CONV_WIDTH, tn), lambda i, j: (0, j)),
            pl.BlockSpec((FFN_CONV_WIDTH, tn), lambda i, j: (0, j + nj)),
            pl.BlockSpec((1, tn), lambda i, j: (0, j)),
            pl.BlockSpec((1, tn), lambda i, j: (0, j + nj)),
        ],
        out_specs=pl.BlockSpec((tm, tn), lambda i, j: (i, j)),
        out_shape=jax.ShapeDtypeStruct((t, D_FF), BF16),
        scratch_shapes=[pltpu.VMEM((tm + halo, D_MODEL), BF16)],
        compiler_params=_params("arbitrary", "arbitrary"),
        name="ffn_up",
    )(x2d, x2d, g_pre, w_up, w_up, w_fconv, w_fconv, b_fconv, b_fconv)


def _down_kernel(a_ref, x_ref, w_ref, g_ref, o_ref):
    y = jnp.dot(a_ref[...], w_ref[...], preferred_element_type=F32)
    o_ref[...] = x_ref[...] + _rms(y, g_ref[...])


def _ffn_down(act, x2d, w_down, g_post):
    t = x2d.shape[0]
    tm = TM_DOWN
    return pl.pallas_call(
        _down_kernel,
        grid=(t // tm,),
        in_specs=[
            pl.BlockSpec((tm, D_FF), lambda i: (i, 0)),
            pl.BlockSpec((tm, D_MODEL), lambda i: (i, 0)),
            _resident((D_FF, D_MODEL), lambda i: (0, 0)),
            pl.BlockSpec((1, D_MODEL), lambda i: (0, 0)),
        ],
        out_specs=pl.BlockSpec((tm, D_MODEL), lambda i: (i, 0)),
        out_shape=jax.ShapeDtypeStruct((t, D_MODEL), F32),
        compiler_params=_params("arbitrary"),
        name="ffn_down",
    )(act, x2d, w_down, g_post)


def _rope_angles(positions):
    assert QK_ROPE_DIM * 2 == LANES
    inv_freq = ROPE_THETA ** (-jnp.arange(0, QK_ROPE_DIM, 2, dtype=F32) / QK_ROPE_DIM)
    ang = positions.astype(F32).reshape(-1, 1) * inv_freq
    return jnp.cos(ang), jnp.sin(ang)


def _block_diag_pairs(w):
    nb, d, _ = w.shape
    w = w.reshape(nb // 2, 2, d, d)
    z = jnp.zeros_like(w[:, 0])
    top = jnp.concatenate([w[:, 0], z], axis=-1)
    bot = jnp.concatenate([z, w[:, 1]], axis=-1)
    return jnp.concatenate([top, bot], axis=-2)


def kernel(x, mem, positions, g_pre_mix, g_post_mix, w_in, g_ckv, w_ukv, w_o_mla, w_conv_lru, b_conv_lru, w_rg, b_rg, w_ig, b_ig, lru_lambda, w_o_lru, w_out, g_pre_x, g_post_x, g_mem, w_cq, w_ck, w_cv, w_co, g_pre_ffn, g_post_ffn, w_up, w_fconv, b_fconv, w_down):
    batch, seq, d = x.shape
    assert d == D_MODEL and g_pre_mix.shape[0] == 1
    t = batch * seq
    x2d = x.reshape(t, d)
    l = 0

    q_scale = (QK_HEAD_DIM ** -0.5) * LOG2E
    wi_t = w_in[l].T
    wq_t = (wi_t[:Q_COLS] * q_scale).astype(BF16)
    kv_end = Q_COLS + KV_LORA_RANK + QK_ROPE_DIM
    wr_t = wi_t[kv_end:].astype(BF16)
    wb_t = jnp.pad(wi_t[Q_COLS:kv_end], ((0, LANES - QK_ROPE_DIM), (0, 0))).astype(BF16)
    wukv = w_ukv[l].reshape(KV_LORA_RANK, MLA_HEADS, QK_NOPE_DIM + V_HEAD_DIM)
    w_uk = wukv[:, :, :QK_NOPE_DIM].reshape(KV_LORA_RANK, -1).astype(BF16)
    w_uvt = wukv[:, :, QK_NOPE_DIM:].reshape(KV_LORA_RANK, -1).T.astype(BF16)
    row = lambda v: v.reshape(1, -1)
    cos, sin = _rope_angles(positions)

    q_t, proj = _proj_in(x2d, row(g_pre_mix[l]), wq_t, wr_t, cos.T, sin.T, batch, seq)
    kn, kr, vt = _kv_path(x2d, row(g_pre_mix[l]), wb_t, row(g_ckv[l]), w_uk, w_uvt, cos, sin,
                          batch, seq)
    proj3 = proj.reshape(batch, seq, -1)
    attn = _mla_attn(q_t, kn.reshape(batch, seq, -1), kr.reshape(batch, seq, LANES),
                     vt.reshape(batch, seq // TM_KV, MLA_HEADS * V_HEAD_DIM, TM_KV), batch, seq)
    lru = _lru(proj3, w_conv_lru[l], row(b_conv_lru[l]),
               _block_diag_pairs(w_rg[l]).astype(BF16), row(b_rg[l]),
               _block_diag_pairs(w_ig[l]).astype(BF16), row(b_ig[l]),
               row(lru_lambda[l]), batch, seq)
    x1 = _merge(attn.reshape(t, -1), lru.reshape(t, -1), proj, x2d,
                w_o_mla[l].astype(BF16), w_o_lru[l].astype(BF16), w_out[l].astype(BF16),
                row(g_post_mix[l]))

    c_scale = (X_HEAD_DIM ** -0.5) * LOG2E
    ckt, cv = _mem_kv(mem, row(g_mem[l]), w_ck[l].T.astype(BF16), w_cv[l].astype(BF16))
    x2 = _cross(x1, row(g_pre_x[l]), (w_cq[l] * c_scale).astype(BF16), ckt, cv,
                w_co[l].astype(BF16), row(g_post_x[l]), seq)

    act = _ffn_up(x2, row(g_pre_ffn[l]), w_up[l].astype(BF16), w_fconv[l], row(b_fconv[l]), seq)
    x3 = _ffn_down(act, x2, w_down[l].astype(BF16), row(g_post_ffn[l]))
    return x3.reshape(batch, seq, d)
```

```python
import functools
import math

import jax
import jax.numpy as jnp
from jax import lax
from jax.experimental import pallas as pl
from jax.experimental.pallas import tpu as pltpu

F32 = jnp.float32
BF16 = jnp.bfloat16

D_MODEL = 2048
MLA_HEADS = 16
QK_NOPE_DIM = 128
QK_ROPE_DIM = 64
V_HEAD_DIM = 128
KV_LORA_RANK = 512
QK_HEAD_DIM = QK_NOPE_DIM + QK_ROPE_DIM
ROPE_THETA = 10000.0
LRU_WIDTH = D_MODEL
LRU_BLOCKS = 16
LRU_BLOCK_DIM = LRU_WIDTH // LRU_BLOCKS
LRU_CONV_WIDTH = 4
LRU_C = 8.0
X_HEADS = 4
X_HEAD_DIM = 128
D_FF = 5632
FFN_CONV_WIDTH = 3
EPS = 1e-6

LOG2E = math.log2(math.e)
LANES = 128
BF16_ROWS = 16
Q_PAD_DIM = 2 * LANES
V_AUG_DIM = V_HEAD_DIM + BF16_ROWS
Q_COLS = MLA_HEADS * QK_HEAD_DIM
VMEM_LIMIT = 56 * 1024 * 1024
NT_DIMS = (((1,), (1,)), ((), ()))

TM_PROJ, TN_PROJ = 1024, 1024
TN_Q = 4 * QK_HEAD_DIM
TM_KV = 512
TQ, TK = 2048, 1024
assert TQ == 2 * TK and TK % TM_KV == 0
Q_CHUNKS = 4
PROBE_KEYS = BF16_ROWS
SCORE_JUMP_LIMIT = 64.0
TS_LRU, CB_LRU = 512, 256
TM_MERGE = 256
TM_CROSS = 512
TM_UP, TN_UP, HALO_UP = 1024, 512, 16
TM_DOWN = 256


def _params(*sem):
    return pltpu.CompilerParams(dimension_semantics=sem, vmem_limit_bytes=VMEM_LIMIT)


def _resident(shape, index_map):
    return pl.BlockSpec(shape, index_map, pipeline_mode=pl.Buffered(1))


def _rms(x, g):
    return x * lax.rsqrt(jnp.mean(x * x, axis=-1, keepdims=True) + EPS) * g


def _gelu_tanh(x):
    c = math.sqrt(2.0 / math.pi)
    return 0.5 * x * (1.0 + jnp.tanh(c * (x + 0.044715 * (x * x * x))))


def _rope_tables(cos_ref, sin_ref):
    c = cos_ref[...]
    s = sin_ref[...]
    z = jnp.zeros_like(c)
    return jnp.concatenate([c, c, z, z, -s, z, z, z, z, s, z, z], axis=-1)


def _rope_slab(slab, trig):
    c = trig[:, 0:LANES]
    s1 = trig[:, LANES:2 * LANES]
    s2 = trig[:, 2 * LANES:3 * LANES]
    return slab * c + pltpu.roll(slab, 96, 1) * s1 + pltpu.roll(slab, 32, 1) * s2


def _proj_kernel(x_ref, g_ref, wq_ref, wr_ref, cos_ref, sin_ref, qt_ref, o_ref, h_ref, *,
                 n_q_tiles):
    j = pl.program_id(1)

    @pl.when(j == 0)
    def _():
        h_ref[...] = _rms(x_ref[...], g_ref[...]).astype(BF16)

    @pl.when(j >= n_q_tiles)
    def _():
        o_ref[...] = lax.dot_general(h_ref[...], wr_ref[...], NT_DIMS,
                                     preferred_element_type=F32).astype(BF16)

    @pl.when(j < n_q_tiles)
    def _():
        acc = lax.dot_general(wq_ref[...], h_ref[...], NT_DIMS, preferred_element_type=F32)
        cos_t = cos_ref[...]
        sin_t = sin_ref[...]
        half = QK_ROPE_DIM // 2
        for hh in range(TN_Q // QK_HEAD_DIM):
            r0 = hh * QK_HEAD_DIM
            r1 = r0 + QK_NOPE_DIM
            qt_ref[r0:r1, :] = acc[r0:r1, :].astype(BF16)
            x1 = acc[r1:r1 + half, :]
            x2 = acc[r1 + half:r1 + 2 * half, :]
            qt_ref[r1:r1 + half, :] = (x1 * cos_t - x2 * sin_t).astype(BF16)
            qt_ref[r1 + half:r1 + 2 * half, :] = (x2 * cos_t + x1 * sin_t).astype(BF16)


def _proj_in(x2d, g, wq_t, wr_t, cos_t, sin_t, batch, seq):
    t = x2d.shape[0]
    nq = wq_t.shape[0] // TN_Q
    nr = wr_t.shape[0] // TN_PROJ
    per_b = seq // TM_PROJ
    half = QK_ROPE_DIM // 2
    return pl.pallas_call(
        functools.partial(_proj_kernel, n_q_tiles=nq),
        grid=(t // TM_PROJ, nq + nr),
        in_specs=[
            pl.BlockSpec((TM_PROJ, D_MODEL), lambda i, j: (i, 0)),
            pl.BlockSpec((1, D_MODEL), lambda i, j: (0, 0)),
            pl.BlockSpec((TN_Q, D_MODEL), lambda i, j: (jnp.minimum(j, nq - 1), 0)),
            pl.BlockSpec((TN_PROJ, D_MODEL), lambda i, j: (jnp.maximum(j - nq, 0), 0)),
            pl.BlockSpec((half, TM_PROJ), lambda i, j: (0, i)),
            pl.BlockSpec((half, TM_PROJ), lambda i, j: (0, i)),
        ],
        out_specs=[
            pl.BlockSpec((None, TN_Q, TM_PROJ),
                         lambda i, j: (i // per_b, jnp.minimum(j, nq - 1), i % per_b)),
            pl.BlockSpec((TM_PROJ, TN_PROJ), lambda i, j: (i, jnp.maximum(j - nq, 0))),
        ],
        out_shape=[
            jax.ShapeDtypeStruct((batch, wq_t.shape[0], seq), BF16),
            jax.ShapeDtypeStruct((t, wr_t.shape[0]), BF16),
        ],
        scratch_shapes=[pltpu.VMEM((TM_PROJ, D_MODEL), BF16)],
        compiler_params=_params("arbitrary", "arbitrary"),
        name="proj_in",
    )(x2d, g, wq_t, wr_t, cos_t, sin_t)


def _kv_kernel(x_ref, g_ref, wb_ref, gc_ref, wuk_ref, wuvt_ref, cos_ref, sin_ref,
               kn_ref, kr_ref, vt_ref):
    h = _rms(x_ref[...], g_ref[...]).astype(BF16)
    c = lax.dot_general(h, wb_ref[...], NT_DIMS, preferred_element_type=F32)
    n = _rms(c[:, :KV_LORA_RANK], gc_ref[...]).astype(BF16)
    kn_ref[...] = jnp.dot(n, wuk_ref[...], preferred_element_type=F32).astype(BF16)
    vt_ref[...] = lax.dot_general(wuvt_ref[...], n, NT_DIMS,
                                  preferred_element_type=F32).astype(BF16)
    kr_ref[...] = _rope_slab(c[:, KV_LORA_RANK:], _rope_tables(cos_ref, sin_ref)).astype(BF16)


def _kv_path(x2d, g, w_b, g_ckv, w_uk, w_uvt, cos, sin, batch, seq):
    t = x2d.shape[0]
    nk = seq // TM_KV
    hv = MLA_HEADS * V_HEAD_DIM
    return pl.pallas_call(
        _kv_kernel,
        grid=(t // TM_KV,),
        in_specs=[
            pl.BlockSpec((TM_KV, D_MODEL), lambda i: (i, 0)),
            pl.BlockSpec((1, D_MODEL), lambda i: (0, 0)),
            _resident(w_b.shape, lambda i: (0, 0)),
            pl.BlockSpec((1, KV_LORA_RANK), lambda i: (0, 0)),
            _resident(w_uk.shape, lambda i: (0, 0)),
            _resident(w_uvt.shape, lambda i: (0, 0)),
            pl.BlockSpec((TM_KV, QK_ROPE_DIM // 2), lambda i: (i, 0)),
            pl.BlockSpec((TM_KV, QK_ROPE_DIM // 2), lambda i: (i, 0)),
        ],
        out_specs=[
            pl.BlockSpec((TM_KV, hv), lambda i: (i, 0)),
            pl.BlockSpec((TM_KV, LANES), lambda i: (i, 0)),
            pl.BlockSpec((None, hv, TM_KV), lambda i: (i, 0, 0)),
        ],
        out_shape=[
            jax.ShapeDtypeStruct((t, hv), BF16),
            jax.ShapeDtypeStruct((t, LANES), BF16),
            jax.ShapeDtypeStruct((batch * nk, hv, TM_KV), BF16),
        ],
        compiler_params=_params("arbitrary"),
        name="kv_path",
    )(x2d, g, w_b, g_ckv, w_uk, w_uvt, cos, sin)


def _attn_kernel(q_ref, kn_ref, kr_ref, vt_ref, o_ref,
                 kcat_ref, vaug_ref, qt_ref, p0, p1, a0, a1, acc_ref, r_ref, rprev_ref, jump_ref):
    i = pl.program_id(2)
    n = (TQ // TK) * (i + 1)

    @pl.when(i == 0)
    def _():
        kcat_ref[:, 0:LANES] = kn_ref[...]
        kcat_ref[:, LANES:Q_PAD_DIM] = kr_ref[...]
        sub = TK // TM_KV
        for j in range(vaug_ref.shape[0]):
            for c in range(sub):
                vaug_ref[j, 0:V_HEAD_DIM, c * TM_KV:(c + 1) * TM_KV] = vt_ref[j * sub + c]
        vaug_ref[:, V_HEAD_DIM:V_AUG_DIM, :] = jnp.ones(
            (vaug_ref.shape[0], V_AUG_DIM - V_HEAD_DIM, TK), BF16)
        qt_ref[QK_HEAD_DIM:Q_PAD_DIM, :] = jnp.zeros((Q_PAD_DIM - QK_HEAD_DIM, TQ), BF16)

    qt_ref[0:QK_HEAD_DIM, :] = q_ref[...]
    acc_ref[...] = jnp.zeros(acc_ref.shape, F32)

    s_probe = jnp.dot(kcat_ref[0:PROBE_KEYS, :], qt_ref[...], preferred_element_type=F32)
    key = lax.broadcasted_iota(jnp.int32, s_probe.shape, 0)
    qry = lax.broadcasted_iota(jnp.int32, s_probe.shape, 1) + i * TQ
    r0 = jnp.max(jnp.where(key <= qry, s_probe, -jnp.inf), axis=0, keepdims=True)
    r_ref[...] = r0
    rprev_ref[...] = r0
    jump_ref[...] = jnp.zeros(jump_ref.shape, F32)

    def visible(diag, cols):
        if diag is None:
            return TK, False
        nkeys = max(0, min(TK, cols.stop - diag * TK))
        return nkeys, diag * TK + nkeys - 1 > cols.start

    def qs(b, p_ref, a_ref, diag, cols):
        nkeys, masked = visible(diag, cols)
        if nkeys == 0:
            return
        k0 = pl.multiple_of(b * TK, TK)
        s = jnp.dot(kcat_ref[pl.ds(k0, nkeys), :], qt_ref[:, cols],
                    preferred_element_type=F32)
        if masked:
            key = lax.broadcasted_iota(jnp.int32, s.shape, 0) + diag * TK
            qry = lax.broadcasted_iota(jnp.int32, s.shape, 1) + cols.start
            s = jnp.where(key <= qry, s, -jnp.inf)
        r = r_ref[:, cols]
        p_ref[0:nkeys, cols] = jnp.exp2(s - r).astype(BF16)
        a_ref[:, cols] = jnp.exp2(rprev_ref[:, cols] - r)
        rprev_ref[:, cols] = r
        block_max = jnp.max(s, axis=0, keepdims=True)
        r_ref[:, cols] = jnp.maximum(r, block_max)
        jump_ref[:, cols] = jnp.maximum(jump_ref[:, cols], block_max - r)

    def pv(b, p_ref, a_ref, diag, cols):
        nkeys, _ = visible(diag, cols)
        if nkeys == 0:
            return
        acc_ref[:, cols] = acc_ref[:, cols] * a_ref[:, cols] + jnp.dot(
            vaug_ref[b, :, 0:nkeys], p_ref[0:nkeys, cols],
            preferred_element_type=F32)

    bufs = ((p0, a0), (p1, a1))
    skip = object()

    def step(t, parity, qs_diag=skip, pv_diag=skip):
        p_c, a_c = bufs[parity]
        p_p, a_p = bufs[1 - parity]
        for c in range(Q_CHUNKS):
            cols = slice(c * (TQ // Q_CHUNKS), (c + 1) * (TQ // Q_CHUNKS))
            if qs_diag is not skip:
                qs(t, p_c, a_c, qs_diag, cols)
            if pv_diag is not skip:
                pv(t - 1, p_p, a_p, pv_diag, cols)

    @pl.when(n == 2)
    def _():
        step(0, 0, qs_diag=0)

    @pl.when(n > 2)
    def _():
        step(0, 0, qs_diag=None)

    def steady(t, carry):
        for parity in (0, 1):
            @pl.when(t % 2 == parity)
            def _():
                step(t, parity, qs_diag=None, pv_diag=None)
        return carry

    lax.fori_loop(1, n - 2, steady, 0)

    @pl.when(n > 2)
    def _():
        step(n - 2, 0, qs_diag=0, pv_diag=None)

    step(n - 1, 1, qs_diag=1, pv_diag=0)
    step(n, 0, pv_diag=1)

    @pl.when(jnp.max(jump_ref[...]) > SCORE_JUMP_LIMIT)
    def _():
        acc_ref[...] = jnp.zeros(acc_ref.shape, F32)

        def block(b, m_old):
            k0 = pl.multiple_of(b * TK, TK)
            s = jnp.dot(kcat_ref[pl.ds(k0, TK), :], qt_ref[...], preferred_element_type=F32)
            key = lax.broadcasted_iota(jnp.int32, s.shape, 0) + b * TK
            qry = lax.broadcasted_iota(jnp.int32, s.shape, 1) + i * TQ
            s = jnp.where(key <= qry, s, -jnp.inf)
            m_new = jnp.maximum(m_old, jnp.max(s, axis=0, keepdims=True))
            p = jnp.exp2(s - m_new).astype(BF16)
            acc_ref[...] = acc_ref[...] * jnp.exp2(m_old - m_new) + jnp.dot(
                vaug_ref[b], p, preferred_element_type=F32)
            return m_new

        lax.fori_loop(0, n, block, jnp.full((1, TQ), -jnp.inf, F32))

    out = acc_ref[0:V_HEAD_DIM, :] * (1.0 / acc_ref[V_HEAD_DIM:V_HEAD_DIM + 1, :])
    o_ref[...] = out.T.astype(BF16)


def _mla_attn(q_t, kn3, kr3, vt4, batch, seq):
    nk = seq // TK
    hv = MLA_HEADS * V_HEAD_DIM
    return pl.pallas_call(
        _attn_kernel,
        grid=(batch, MLA_HEADS, seq // TQ),
        in_specs=[
            pl.BlockSpec((None, QK_HEAD_DIM, TQ), lambda b, h, i: (b, h, i)),
            pl.BlockSpec((None, seq, LANES), lambda b, h, i: (b, 0, h)),
            pl.BlockSpec((None, seq, LANES), lambda b, h, i: (b, 0, 0)),
            pl.BlockSpec((None, seq // TM_KV, V_HEAD_DIM, TM_KV), lambda b, h, i: (b, 0, h, 0)),
        ],
        out_specs=pl.BlockSpec((None, TQ, V_HEAD_DIM), lambda b, h, i: (b, i, h)),
        out_shape=jax.ShapeDtypeStruct((batch, seq, hv), BF16),
        scratch_shapes=[
            pltpu.VMEM((seq, Q_PAD_DIM), BF16),
            pltpu.VMEM((nk, V_AUG_DIM, TK), BF16),
            pltpu.VMEM((Q_PAD_DIM, TQ), BF16),
            pltpu.VMEM((TK, TQ), BF16),
            pltpu.VMEM((TK, TQ), BF16),
            pltpu.VMEM((1, TQ), F32),
            pltpu.VMEM((1, TQ), F32),
            pltpu.VMEM((V_AUG_DIM, TQ), F32),
            pltpu.VMEM((1, TQ), F32),
            pltpu.VMEM((1, TQ), F32),
            pltpu.VMEM((1, TQ), F32),
        ],
        compiler_params=_params("arbitrary", "arbitrary", "arbitrary"),
        name="mla_attn",
    )(q_t, kn3, kr3, vt4)


def _lru_kernel(x_ref, y_ref, wc_ref, bc_ref, wrg_ref, brg_ref, wig_ref, big_ref, lam_ref,
                o_ref, xprev, a_s, b_s, hc_ref):
    si = pl.program_id(2)
    ts, cb = TS_LRU, CB_LRU

    @pl.when(si == 0)
    def _():
        xprev[...] = jnp.zeros((8, cb), F32)
        hc_ref[...] = jnp.zeros((8, cb), F32)

    x = x_ref[...].astype(F32)
    prev = xprev[...]
    row8 = lax.broadcasted_iota(jnp.int32, (8, cb), 0)
    last = LRU_CONV_WIDTH - 1
    xc = bc_ref[...] + x * wc_ref[last:last + 1, :]
    for k in range(1, LRU_CONV_WIDTH):
        xs = pltpu.roll(x, k, 0)
        head = jnp.where(row8 < k, pltpu.roll(prev, k, 0), xs[0:8, :])
        xs = jnp.concatenate([head, xs[8:, :]], axis=0)
        xc = xc + xs * wc_ref[last - k:last - k + 1, :]
    xprev[...] = x[ts - 8:, :]

    xcb = xc.astype(BF16)
    r = jax.nn.sigmoid(jnp.dot(xcb, wrg_ref[...], preferred_element_type=F32) + brg_ref[...])
    ig = jax.nn.sigmoid(jnp.dot(xcb, wig_ref[...], preferred_element_type=F32) + big_ref[...])
    z = -lam_ref[...]
    softplus = jnp.maximum(z, 0.0) + jnp.log1p(jnp.exp(-jnp.abs(z)))
    a = jnp.exp((-LRU_C * softplus) * r)
    v = 1.0 - a * a
    u = jnp.where(v > 0.0, v * lax.rsqrt(v), 0.0) * (ig * xc)

    ng = ts // 8
    a = a.reshape(ng, 8, cb)
    u = u.reshape(ng, 8, cb)
    row = lax.broadcasted_iota(jnp.int32, (ng, 8, cb), 1)
    for d in (1, 2, 4):
        keep = row >= d
        a_sh = pltpu.roll(a, d, 1)
        u_sh = pltpu.roll(u, d, 1)
        u = jnp.where(keep, a, 0.0) * u_sh + u
        a = a * jnp.where(keep, a_sh, 1.0)
    a_s[...] = a.reshape(ts, cb)
    b_s[...] = u.reshape(ts, cb)

    hc = hc_ref[...]
    for g in range(ts // 8):
        hg = a_s[8 * g:8 * g + 8, :] * hc + b_s[8 * g:8 * g + 8, :]
        b_s[8 * g:8 * g + 8, :] = hg
        hc = jnp.broadcast_to(hg[7:8, :], (8, cb))
    hc_ref[...] = hc

    o_ref[...] = (b_s[...] * _gelu_tanh(y_ref[...].astype(F32))).astype(BF16)


def _lru(proj3, w_conv, b_conv, w_rg2, b_rg, w_ig2, b_ig, lam, batch, seq):
    ts, cb = TS_LRU, CB_LRU
    ncb = LRU_WIDTH // cb
    x_off = 0
    y_off = x_off + ncb
    return pl.pallas_call(
        _lru_kernel,
        grid=(batch, ncb, seq // ts),
        in_specs=[
            pl.BlockSpec((None, ts, cb), lambda b, c, s: (b, s, x_off + c)),
            pl.BlockSpec((None, ts, cb), lambda b, c, s: (b, s, y_off + c)),
            pl.BlockSpec((LRU_CONV_WIDTH, cb), lambda b, c, s: (0, c)),
            pl.BlockSpec((1, cb), lambda b, c, s: (0, c)),
            pl.BlockSpec((None, cb, cb), lambda b, c, s: (c, 0, 0)),
            pl.BlockSpec((1, cb), lambda b, c, s: (0, c)),
            pl.BlockSpec((None, cb, cb), lambda b, c, s: (c, 0, 0)),
            pl.BlockSpec((1, cb), lambda b, c, s: (0, c)),
            pl.BlockSpec((1, cb), lambda b, c, s: (0, c)),
        ],
        out_specs=pl.BlockSpec((None, ts, cb), lambda b, c, s: (b, s, c)),
        out_shape=jax.ShapeDtypeStruct((batch, seq, LRU_WIDTH), BF16),
        scratch_shapes=[
            pltpu.VMEM((8, cb), F32),
            pltpu.VMEM((ts, cb), F32),
            pltpu.VMEM((ts, cb), F32),
            pltpu.VMEM((8, cb), F32),
        ],
        compiler_params=_params("arbitrary", "arbitrary", "arbitrary"),
        name="lru",
    )(proj3, proj3, w_conv, b_conv, w_rg2, b_rg, w_ig2, b_ig, lam)


def _merge_kernel(attn_ref, lru_ref, gm_ref, gl_ref, x_ref, wom_ref, wol_ref, wout_ref, g_ref,
                  o_ref):
    o_mla = jnp.dot(attn_ref[...], wom_ref[...], preferred_element_type=F32)
    o_lru = jnp.dot(lru_ref[...], wol_ref[...], preferred_element_type=F32)
    merged = (jax.nn.sigmoid(gm_ref[...].astype(F32)) * o_mla
              + jax.nn.sigmoid(gl_ref[...].astype(F32)) * o_lru)
    y = jnp.dot(merged.astype(BF16), wout_ref[...], preferred_element_type=F32)
    o_ref[...] = x_ref[...] + _rms(y, g_ref[...])


def _merge(attn2, lru2, proj2, x2d, w_om, w_ol, w_out, g_post):
    t = x2d.shape[0]
    tm = TM_MERGE
    gm_off = 2 * LRU_WIDTH // D_MODEL
    row = lambda i: (i, 0)
    const = lambda i: (0, 0)
    return pl.pallas_call(
        _merge_kernel,
        grid=(t // tm,),
        in_specs=[
            pl.BlockSpec((tm, D_MODEL), row),
            pl.BlockSpec((tm, D_MODEL), row),
            pl.BlockSpec((tm, D_MODEL), lambda i: (i, gm_off)),
            pl.BlockSpec((tm, D_MODEL), lambda i: (i, gm_off + 1)),
            pl.BlockSpec((tm, D_MODEL), row),
            _resident((D_MODEL, D_MODEL), const),
            _resident((D_MODEL, D_MODEL), const),
            _resident((D_MODEL, D_MODEL), const),
            pl.BlockSpec((1, D_MODEL), const),
        ],
        out_specs=pl.BlockSpec((tm, D_MODEL), row),
        out_shape=jax.ShapeDtypeStruct((t, D_MODEL), F32),
        compiler_params=_params("arbitrary"),
        name="merge",
    )(attn2, lru2, proj2, proj2, x2d, w_om, w_ol, w_out, g_post)


def _memkv_kernel(mem_ref, g_ref, wckt_ref, wcv_ref, ckt_ref, cv_ref):
    mn = _rms(mem_ref[...], g_ref[...]).astype(BF16)
    ckt_ref[...] = lax.dot_general(wckt_ref[...], mn, (((1,), (1,)), ((), ())),
                                   preferred_element_type=F32).astype(BF16)
    cv_ref[...] = jnp.dot(mn, wcv_ref[...], preferred_element_type=F32).astype(BF16)


def _mem_kv(mem, g_mem, w_ckt, w_cv):
    b, m, _ = mem.shape
    xd = X_HEADS * X_HEAD_DIM
    return pl.pallas_call(
        _memkv_kernel,
        grid=(b,),
        in_specs=[
            pl.BlockSpec((None, m, D_MODEL), lambda i: (i, 0, 0)),
            pl.BlockSpec((1, D_MODEL), lambda i: (0, 0)),
            pl.BlockSpec((xd, D_MODEL), lambda i: (0, 0)),
            pl.BlockSpec((D_MODEL, xd), lambda i: (0, 0)),
        ],
        out_specs=[
            pl.BlockSpec((None, xd, m), lambda i: (i, 0, 0)),
            pl.BlockSpec((None, m, xd), lambda i: (i, 0, 0)),
        ],
        out_shape=[
            jax.ShapeDtypeStruct((b, xd, m), BF16),
            jax.ShapeDtypeStruct((b, m, xd), BF16),
        ],
        compiler_params=_params("arbitrary"),
        name="mem_kv",
    )(mem, g_mem, w_ckt, w_cv)


def _cross_kernel(x_ref, gpre_ref, wcq_ref, ckt_ref, cv_ref, wco_ref, gpost_ref, o_ref):
    x = x_ref[...]
    h = _rms(x, gpre_ref[...]).astype(BF16)
    cq = jnp.dot(h, wcq_ref[...], preferred_element_type=F32).astype(BF16)
    outs = []
    for hh in range(X_HEADS):
        sl = slice(hh * X_HEAD_DIM, (hh + 1) * X_HEAD_DIM)
        s = jnp.dot(cq[:, sl], ckt_ref[sl, :], preferred_element_type=F32)
        p = jnp.exp2(s - jnp.max(s, axis=-1, keepdims=True))
        l = jnp.sum(p, axis=-1, keepdims=True)
        pv = jnp.dot(p.astype(BF16), cv_ref[:, sl], preferred_element_type=F32)
        outs.append((pv * (1.0 / l)).astype(BF16))
    co = jnp.concatenate(outs, axis=-1)
    y = jnp.dot(co, wco_ref[...], preferred_element_type=F32)
    o_ref[...] = x + _rms(y, gpost_ref[...])


def _cross(x2d, g_pre, w_cq, ckt, cv, w_co, g_post, seq):
    t = x2d.shape[0]
    tm = TM_CROSS
    xd = X_HEADS * X_HEAD_DIM
    m = cv.shape[1]
    per_b = seq // tm
    return pl.pallas_call(
        _cross_kernel,
        grid=(t // tm,),
        in_specs=[
            pl.BlockSpec((tm, D_MODEL), lambda i: (i, 0)),
            pl.BlockSpec((1, D_MODEL), lambda i: (0, 0)),
            pl.BlockSpec((D_MODEL, xd), lambda i: (0, 0)),
            pl.BlockSpec((None, xd, m), lambda i: (i // per_b, 0, 0)),
            pl.BlockSpec((None, m, xd), lambda i: (i // per_b, 0, 0)),
            pl.BlockSpec((xd, D_MODEL), lambda i: (0, 0)),
            pl.BlockSpec((1, D_MODEL), lambda i: (0, 0)),
        ],
        out_specs=pl.BlockSpec((tm, D_MODEL), lambda i: (i, 0)),
        out_shape=jax.ShapeDtypeStruct((t, D_MODEL), F32),
        compiler_params=_params("arbitrary"),
        name="cross",
    )(x2d, g_pre, w_cq, ckt, cv, w_co, g_post)


def _up_kernel(x_ref, halo_ref, g_ref, wg_ref, wv_ref, cwg_ref, cwv_ref, cbg_ref, cbv_ref,
               o_ref, h_ref, *, tiles_per_seq):
    i = pl.program_id(0)
    j = pl.program_id(1)
    tm, halo = TM_UP, HALO_UP

    @pl.when(j == 0)
    def _():
        g = g_ref[...]
        keep = jnp.where(i % tiles_per_seq == 0, 0.0, 1.0)
        h_ref[0:halo, :] = (_rms(halo_ref[...], g) * keep).astype(BF16)
        h_ref[halo:halo + tm, :] = _rms(x_ref[...], g).astype(BF16)

    def conv(w_ref, cw_ref, cb_ref):
        a = jnp.dot(h_ref[...], w_ref[...], preferred_element_type=F32)
        y = a * cw_ref[0:1, :]
        for k in range(1, FFN_CONV_WIDTH):
            y = pltpu.roll(y, 1, 0) + a * cw_ref[k:k + 1, :]
        return y[halo:, :] + cb_ref[...]

    gate = conv(wg_ref, cwg_ref, cbg_ref)
    val = conv(wv_ref, cwv_ref, cbv_ref)
    o_ref[...] = (_gelu_tanh(gate) * val).astype(BF16)


def _ffn_up(x2d, g_pre, w_up, w_fconv, b_fconv, seq):
    t = x2d.shape[0]
    tm, tn, halo = TM_UP, TN_UP, HALO_UP
    nj = D_FF // tn
    return pl.pallas_call(
        functools.partial(_up_kernel, tiles_per_seq=seq // tm),
        grid=(t // tm, nj),
        in_specs=[
            pl.BlockSpec((tm, D_MODEL), lambda i, j: (i, 0)),
            pl.BlockSpec((halo, D_MODEL), lambda i, j: (jnp.maximum(i * (tm // halo) - 1, 0), 0)),
            pl.BlockSpec((1, D_MODEL), lambda i, j: (0, 0)),
            pl.BlockSpec((D_MODEL, tn), lambda i, j: (0, j)),
            pl.BlockSpec((D_MODEL, tn), lambda i, j: (0, j + nj)),
            pl.BlockSpec((FFN_CONV_WIDTH, tn), lambda i, j: (0, j)),
            pl.BlockSpec((FFN_CONV_WIDTH, tn), lambda i, j: (0, j + nj)),
            pl.BlockSpec((1, tn), lambda i, j: (0, j)),
            pl.BlockSpec((1, tn), lambda i, j: (0, j + nj)),
        ],
        out_specs=pl.BlockSpec((tm, tn), lambda i, j: (i, j)),
        out_shape=jax.ShapeDtypeStruct((t, D_FF), BF16),
        scratch_shapes=[pltpu.VMEM((tm + halo, D_MODEL), BF16)],
        compiler_params=_params("arbitrary", "arbitrary"),
        name="ffn_up",
    )(x2d, x2d, g_pre, w_up, w_up, w_fconv, w_fconv, b_fconv, b_fconv)


def _down_kernel(a_ref, x_ref, w_ref, g_ref, o_ref):
    y = jnp.dot(a_ref[...], w_ref[...], preferred_element_type=F32)
    o_ref[...] = x_ref[...] + _rms(y, g_ref[...])


def _ffn_down(act, x2d, w_down, g_post):
    t = x2d.shape[0]
    tm = TM_DOWN
    return pl.pallas_call(
        _down_kernel,
        grid=(t // tm,),
        in_specs=[
            pl.BlockSpec((tm, D_FF), lambda i: (i, 0)),
            pl.BlockSpec((tm, D_MODEL), lambda i: (i, 0)),
            _resident((D_FF, D_MODEL), lambda i: (0, 0)),
            pl.BlockSpec((1, D_MODEL), lambda i: (0, 0)),
        ],
        out_specs=pl.BlockSpec((tm, D_MODEL), lambda i: (i, 0)),
        out_shape=jax.ShapeDtypeStruct((t, D_MODEL), F32),
        compiler_params=_params("arbitrary"),
        name="ffn_down",
    )(act, x2d, w_down, g_post)


def _rope_angles(positions):
    assert QK_ROPE_DIM * 2 == LANES
    inv_freq = ROPE_THETA ** (-jnp.arange(0, QK_ROPE_DIM, 2, dtype=F32) / QK_ROPE_DIM)
    ang = positions.astype(F32).reshape(-1, 1) * inv_freq
    return jnp.cos(ang), jnp.sin(ang)


def _block_diag_pairs(w):
    nb, d, _ = w.shape
    w = w.reshape(nb // 2, 2, d, d)
    z = jnp.zeros_like(w[:, 0])
    top = jnp.concatenate([w[:, 0], z], axis=-1)
    bot = jnp.concatenate([z, w[:, 1]], axis=-1)
    return jnp.concatenate([top, bot], axis=-2)


def kernel(x, mem, positions, g_pre_mix, g_post_mix, w_in, g_ckv, w_ukv, w_o_mla, w_conv_lru, b_conv_lru, w_rg, b_rg, w_ig, b_ig, lru_lambda, w_o_lru, w_out, g_pre_x, g_post_x, g_mem, w_cq, w_ck, w_cv, w_co, g_pre_ffn, g_post_ffn, w_up, w_fconv, b_fconv, w_down):
    batch, seq, d = x.shape
    assert d == D_MODEL and g_pre_mix.shape[0] == 1
    t = batch * seq
    x2d = x.reshape(t, d)
    l = 0

    q_scale = (QK_HEAD_DIM ** -0.5) * LOG2E
    wi_t = w_in[l].T
    wq_t = (wi_t[:Q_COLS] * q_scale).astype(BF16)
    kv_end = Q_COLS + KV_LORA_RANK + QK_ROPE_DIM
    wr_t = wi_t[kv_end:].astype(BF16)
    wb_t = jnp.pad(wi_t[Q_COLS:kv_end], ((0, LANES - QK_ROPE_DIM), (0, 0))).astype(BF16)
    wukv = w_ukv[l].reshape(KV_LORA_RANK, MLA_HEADS, QK_NOPE_DIM + V_HEAD_DIM)
    w_uk = wukv[:, :, :QK_NOPE_DIM].reshape(KV_LORA_RANK, -1).astype(BF16)
    w_uvt = wukv[:, :, QK_NOPE_DIM:].reshape(KV_LORA_RANK, -1).T.astype(BF16)
    row = lambda v: v.reshape(1, -1)
    cos, sin = _rope_angles(positions)

    q_t, proj = _proj_in(x2d, row(g_pre_mix[l]), wq_t, wr_t, cos.T, sin.T, batch, seq)
    kn, kr, vt = _kv_path(x2d, row(g_pre_mix[l]), wb_t, row(g_ckv[l]), w_uk, w_uvt, cos, sin,
                          batch, seq)
    proj3 = proj.reshape(batch, seq, -1)
    attn = _mla_attn(q_t, kn.reshape(batch, seq, -1), kr.reshape(batch, seq, LANES),
                     vt.reshape(batch, seq // TM_KV, MLA_HEADS * V_HEAD_DIM, TM_KV), batch, seq)
    lru = _lru(proj3, w_conv_lru[l], row(b_conv_lru[l]),
               _block_diag_pairs(w_rg[l]).astype(BF16), row(b_rg[l]),
               _block_diag_pairs(w_ig[l]).astype(BF16), row(b_ig[l]),
               row(lru_lambda[l]), batch, seq)
    x1 = _merge(attn.reshape(t, -1), lru.reshape(t, -1), proj, x2d,
                w_o_mla[l].astype(BF16), w_o_lru[l].astype(BF16), w_out[l].astype(BF16),
                row(g_post_mix[l]))

    c_scale = (X_HEAD_DIM ** -0.5) * LOG2E
    ckt, cv = _mem_kv(mem, row(g_mem[l]), w_ck[l].T.astype(BF16), w_cv[l].astype(BF16))
    x2 = _cross(x1, row(g_pre_x[l]), (w_cq[l] * c_scale).astype(BF16), ckt, cv,
                w_co[l].astype(BF16), row(g_post_x[l]), seq)

    act = _ffn_up(x2, row(g_pre_ffn[l]), w_up[l].astype(BF16), w_fconv[l], row(b_fconv[l]), seq)
    x3 = _ffn_down(act, x2, w_down[l].astype(BF16), row(g_post_ffn[l]))
    return x3.reshape(batch, seq, d)
```

```python
import functools
import math

import jax
import jax.numpy as jnp
from jax import lax
from jax.experimental import pallas as pl
from jax.experimental.pallas import tpu as pltpu

F32 = jnp.float32
BF16 = jnp.bfloat16

D_MODEL = 2048
MLA_HEADS = 16
QK_NOPE_DIM = 128
QK_ROPE_DIM = 64
V_HEAD_DIM = 128
KV_LORA_RANK = 512
QK_HEAD_DIM = QK_NOPE_DIM + QK_ROPE_DIM
ROPE_THETA = 10000.0
LRU_WIDTH = D_MODEL
LRU_BLOCKS = 16
LRU_BLOCK_DIM = LRU_WIDTH // LRU_BLOCKS
LRU_CONV_WIDTH = 4
LRU_C = 8.0
X_HEADS = 4
X_HEAD_DIM = 128
D_FF = 5632
FFN_CONV_WIDTH = 3
EPS = 1e-6

LOG2E = math.log2(math.e)
LANES = 128
BF16_ROWS = 16
Q_PAD_DIM = 2 * LANES
V_AUG_DIM = V_HEAD_DIM + BF16_ROWS
Q_COLS = MLA_HEADS * QK_HEAD_DIM
VMEM_LIMIT = 56 * 1024 * 1024
NT_DIMS = (((1,), (1,)), ((), ()))

TM_PROJ, TN_PROJ = 1024, 1024
TN_Q = 4 * QK_HEAD_DIM
TM_KV = 512
TQ, TK = 4096, 1024
DIAG_BLOCKS = TQ // TK
assert TQ % TK == 0 and DIAG_BLOCKS % 2 == 0 and TK % TM_KV == 0
Q_CHUNKS = 8
PROBE_KEYS = BF16_ROWS
SCORE_JUMP_LIMIT = 64.0
TS_LRU, CB_LRU = 512, 256
TM_MERGE = 256
TM_CROSS = 512
TM_UP, TN_UP, HALO_UP = 1024, 512, 16
TM_DOWN = 256


def _params(*sem):
    return pltpu.CompilerParams(dimension_semantics=sem, vmem_limit_bytes=VMEM_LIMIT)


def _resident(shape, index_map):
    return pl.BlockSpec(shape, index_map, pipeline_mode=pl.Buffered(1))


def _rms(x, g):
    return x * lax.rsqrt(jnp.mean(x * x, axis=-1, keepdims=True) + EPS) * g


def _gelu_tanh(x):
    c = math.sqrt(2.0 / math.pi)
    return 0.5 * x * (1.0 + jnp.tanh(c * (x + 0.044715 * (x * x * x))))


def _rope_tables(cos_ref, sin_ref):
    c = cos_ref[...]
    s = sin_ref[...]
    z = jnp.zeros_like(c)
    return jnp.concatenate([c, c, z, z, -s, z, z, z, z, s, z, z], axis=-1)


def _rope_slab(slab, trig):
    c = trig[:, 0:LANES]
    s1 = trig[:, LANES:2 * LANES]
    s2 = trig[:, 2 * LANES:3 * LANES]
    return slab * c + pltpu.roll(slab, 96, 1) * s1 + pltpu.roll(slab, 32, 1) * s2


def _proj_kernel(x_ref, g_ref, wq_ref, wr_ref, cos_ref, sin_ref, qt_ref, o_ref, h_ref, *,
                 n_q_tiles):
    j = pl.program_id(1)

    @pl.when(j == 0)
    def _():
        h_ref[...] = _rms(x_ref[...], g_ref[...]).astype(BF16)

    @pl.when(j >= n_q_tiles)
    def _():
        o_ref[...] = lax.dot_general(h_ref[...], wr_ref[...], NT_DIMS,
                                     preferred_element_type=F32).astype(BF16)

    @pl.when(j < n_q_tiles)
    def _():
        acc = lax.dot_general(wq_ref[...], h_ref[...], NT_DIMS, preferred_element_type=F32)
        cos_t = cos_ref[...]
        sin_t = sin_ref[...]
        half = QK_ROPE_DIM // 2
        for hh in range(TN_Q // QK_HEAD_DIM):
            r0 = hh * QK_HEAD_DIM
            r1 = r0 + QK_NOPE_DIM
            qt_ref[r0:r1, :] = acc[r0:r1, :].astype(BF16)
            x1 = acc[r1:r1 + half, :]
            x2 = acc[r1 + half:r1 + 2 * half, :]
            qt_ref[r1:r1 + half, :] = (x1 * cos_t - x2 * sin_t).astype(BF16)
            qt_ref[r1 + half:r1 + 2 * half, :] = (x2 * cos_t + x1 * sin_t).astype(BF16)


def _proj_in(x2d, g, wq_t, wr_t, cos_t, sin_t, batch, seq):
    t = x2d.shape[0]
    nq = wq_t.shape[0] // TN_Q
    nr = wr_t.shape[0] // TN_PROJ
    per_b = seq // TM_PROJ
    half = QK_ROPE_DIM // 2
    return pl.pallas_call(
        functools.partial(_proj_kernel, n_q_tiles=nq),
        grid=(t // TM_PROJ, nq + nr),
        in_specs=[
            pl.BlockSpec((TM_PROJ, D_MODEL), lambda i, j: (i, 0)),
            pl.BlockSpec((1, D_MODEL), lambda i, j: (0, 0)),
            pl.BlockSpec((TN_Q, D_MODEL), lambda i, j: (jnp.minimum(j, nq - 1), 0)),
            pl.BlockSpec((TN_PROJ, D_MODEL), lambda i, j: (jnp.maximum(j - nq, 0), 0)),
            pl.BlockSpec((half, TM_PROJ), lambda i, j: (0, i)),
            pl.BlockSpec((half, TM_PROJ), lambda i, j: (0, i)),
        ],
        out_specs=[
            pl.BlockSpec((None, TN_Q, TM_PROJ),
                         lambda i, j: (i // per_b, jnp.minimum(j, nq - 1), i % per_b)),
            pl.BlockSpec((TM_PROJ, TN_PROJ), lambda i, j: (i, jnp.maximum(j - nq, 0))),
        ],
        out_shape=[
            jax.ShapeDtypeStruct((batch, wq_t.shape[0], seq), BF16),
            jax.ShapeDtypeStruct((t, wr_t.shape[0]), BF16),
        ],
        scratch_shapes=[pltpu.VMEM((TM_PROJ, D_MODEL), BF16)],
        compiler_params=_params("arbitrary", "arbitrary"),
        name="proj_in",
    )(x2d, g, wq_t, wr_t, cos_t, sin_t)


def _kv_kernel(x_ref, g_ref, wb_ref, gc_ref, wuk_ref, wuvt_ref, cos_ref, sin_ref,
               kn_ref, kr_ref, vt_ref):
    h = _rms(x_ref[...], g_ref[...]).astype(BF16)
    c = lax.dot_general(h, wb_ref[...], NT_DIMS, preferred_element_type=F32)
    n = _rms(c[:, :KV_LORA_RANK], gc_ref[...]).astype(BF16)
    kn_ref[...] = jnp.dot(n, wuk_ref[...], preferred_element_type=F32).astype(BF16)
    vt_ref[...] = lax.dot_general(wuvt_ref[...], n, NT_DIMS,
                                  preferred_element_type=F32).astype(BF16)
    kr_ref[...] = _rope_slab(c[:, KV_LORA_RANK:], _rope_tables(cos_ref, sin_ref)).astype(BF16)


def _kv_path(x2d, g, w_b, g_ckv, w_uk, w_uvt, cos, sin, batch, seq):
    t = x2d.shape[0]
    nk = seq // TM_KV
    hv = MLA_HEADS * V_HEAD_DIM
    return pl.pallas_call(
        _kv_kernel,
        grid=(t // TM_KV,),
        in_specs=[
            pl.BlockSpec((TM_KV, D_MODEL), lambda i: (i, 0)),
            pl.BlockSpec((1, D_MODEL), lambda i: (0, 0)),
            _resident(w_b.shape, lambda i: (0, 0)),
            pl.BlockSpec((1, KV_LORA_RANK), lambda i: (0, 0)),
            _resident(w_uk.shape, lambda i: (0, 0)),
            _resident(w_uvt.shape, lambda i: (0, 0)),
            pl.BlockSpec((TM_KV, QK_ROPE_DIM // 2), lambda i: (i, 0)),
            pl.BlockSpec((TM_KV, QK_ROPE_DIM // 2), lambda i: (i, 0)),
        ],
        out_specs=[
            pl.BlockSpec((TM_KV, hv), lambda i: (i, 0)),
            pl.BlockSpec((TM_KV, LANES), lambda i: (i, 0)),
            pl.BlockSpec((None, hv, TM_KV), lambda i: (i, 0, 0)),
        ],
        out_shape=[
            jax.ShapeDtypeStruct((t, hv), BF16),
            jax.ShapeDtypeStruct((t, LANES), BF16),
            jax.ShapeDtypeStruct((batch * nk, hv, TM_KV), BF16),
        ],
        compiler_params=_params("arbitrary"),
        name="kv_path",
    )(x2d, g, w_b, g_ckv, w_uk, w_uvt, cos, sin)


def _attn_kernel(q_ref, kn_ref, kr_ref, vt_ref, o_ref,
                 kcat_ref, vaug_ref, qt_ref, p0, p1, a0, a1, acc_ref, r_ref, rprev_ref, jump_ref):
    i = pl.program_id(2)
    n = DIAG_BLOCKS * (i + 1)

    @pl.when(i == 0)
    def _():
        kcat_ref[:, 0:LANES] = kn_ref[...]
        kcat_ref[:, LANES:Q_PAD_DIM] = kr_ref[...]
        sub = TK // TM_KV
        for j in range(vaug_ref.shape[0]):
            for c in range(sub):
                vaug_ref[j, 0:V_HEAD_DIM, c * TM_KV:(c + 1) * TM_KV] = vt_ref[j * sub + c]
        vaug_ref[:, V_HEAD_DIM:V_AUG_DIM, :] = jnp.ones(
            (vaug_ref.shape[0], V_AUG_DIM - V_HEAD_DIM, TK), BF16)
        qt_ref[:, QK_HEAD_DIM:Q_PAD_DIM, :] = jnp.zeros(
            (DIAG_BLOCKS, Q_PAD_DIM - QK_HEAD_DIM, TK), BF16)

    acc_ref[...] = jnp.zeros(acc_ref.shape, F32)
    jump_ref[...] = jnp.zeros(jump_ref.shape, F32)
    for c in range(DIAG_BLOCKS):
        slab = slice(c * TK, (c + 1) * TK)
        qt_ref[c, 0:QK_HEAD_DIM, :] = q_ref[:, slab]
        s_probe = jnp.dot(kcat_ref[0:PROBE_KEYS, :], qt_ref[c], preferred_element_type=F32)
        key = lax.broadcasted_iota(jnp.int32, s_probe.shape, 0)
        qry = lax.broadcasted_iota(jnp.int32, s_probe.shape, 1) + (i * TQ + c * TK)
        r0 = jnp.max(jnp.where(key <= qry, s_probe, -jnp.inf), axis=0, keepdims=True)
        r_ref[:, slab] = r0
        rprev_ref[:, slab] = r0

    def in_slab(cols):
        c = cols.start // TK
        return c, slice(cols.start - c * TK, cols.stop - c * TK)

    def visible(diag, cols):
        if diag is None:
            return TK, False
        nkeys = max(0, min(TK, cols.stop - diag * TK))
        return nkeys, diag * TK + nkeys - 1 > cols.start

    def qs(b, p_ref, a_ref, diag, cols):
        nkeys, masked = visible(diag, cols)
        if nkeys == 0:
            return
        k0 = pl.multiple_of(b * TK, TK)
        c, sub = in_slab(cols)
        s = jnp.dot(kcat_ref[pl.ds(k0, nkeys), :], qt_ref[c, :, sub],
                    preferred_element_type=F32)
        if masked:
            key = lax.broadcasted_iota(jnp.int32, s.shape, 0) + diag * TK
            qry = lax.broadcasted_iota(jnp.int32, s.shape, 1) + cols.start
            s = jnp.where(key <= qry, s, -jnp.inf)
        r = r_ref[:, cols]
        p_ref[0:nkeys, cols] = jnp.exp2(s - r).astype(BF16)
        a_ref[:, cols] = jnp.exp2(rprev_ref[:, cols] - r)
        rprev_ref[:, cols] = r
        block_max = jnp.max(s, axis=0, keepdims=True)
        r_ref[:, cols] = jnp.maximum(r, block_max)
        jump_ref[:, cols] = jnp.maximum(jump_ref[:, cols], block_max - r)

    def pv(b, p_ref, a_ref, diag, cols):
        nkeys, _ = visible(diag, cols)
        if nkeys == 0:
            return
        c, sub = in_slab(cols)
        acc_ref[c, :, sub] = acc_ref[c, :, sub] * a_ref[:, cols] + jnp.dot(
            vaug_ref[b, :, 0:nkeys], p_ref[0:nkeys, cols],
            preferred_element_type=F32)

    bufs = ((p0, a0), (p1, a1))
    skip = object()

    def step(t, parity, qs_diag=skip, pv_diag=skip):
        p_c, a_c = bufs[parity]
        p_p, a_p = bufs[1 - parity]
        for c in range(Q_CHUNKS):
            cols = slice(c * (TQ // Q_CHUNKS), (c + 1) * (TQ // Q_CHUNKS))
            if qs_diag is not skip:
                qs(t, p_c, a_c, qs_diag, cols)
            if pv_diag is not skip:
                pv(t - 1, p_p, a_p, pv_diag, cols)

    @pl.when(n == DIAG_BLOCKS)
    def _():
        step(0, 0, qs_diag=0)

    @pl.when(n > DIAG_BLOCKS)
    def _():
        step(0, 0, qs_diag=None)

    def steady(t, carry):
        for parity in (0, 1):
            @pl.when(t % 2 == parity)
            def _():
                step(t, parity, qs_diag=None, pv_diag=None)
        return carry

    lax.fori_loop(1, n - DIAG_BLOCKS, steady, 0)

    @pl.when(n > DIAG_BLOCKS)
    def _():
        step(n - DIAG_BLOCKS, 0, qs_diag=0, pv_diag=None)

    for d in range(1, DIAG_BLOCKS):
        step(n - DIAG_BLOCKS + d, d % 2, qs_diag=d, pv_diag=d - 1)
    step(n, 0, pv_diag=DIAG_BLOCKS - 1)

    @pl.when(jnp.max(jump_ref[...]) > SCORE_JUMP_LIMIT)
    def _():
        acc_ref[...] = jnp.zeros(acc_ref.shape, F32)

        def slab(c, carry):
            def block(b, m_old):
                k0 = pl.multiple_of(b * TK, TK)
                s = jnp.dot(kcat_ref[pl.ds(k0, TK), :], qt_ref[c], preferred_element_type=F32)
                key = lax.broadcasted_iota(jnp.int32, s.shape, 0) + b * TK
                qry = lax.broadcasted_iota(jnp.int32, s.shape, 1) + (i * TQ + c * TK)
                s = jnp.where(key <= qry, s, -jnp.inf)
                m_new = jnp.maximum(m_old, jnp.max(s, axis=0, keepdims=True))
                p = jnp.exp2(s - m_new).astype(BF16)
                acc_ref[c] = acc_ref[c] * jnp.exp2(m_old - m_new) + jnp.dot(
                    vaug_ref[b], p, preferred_element_type=F32)
                return m_new

            lax.fori_loop(0, n, block, jnp.full((1, TK), -jnp.inf, F32))
            return carry

        lax.fori_loop(0, DIAG_BLOCKS, slab, 0)

    for c in range(DIAG_BLOCKS):
        out = acc_ref[c, 0:V_HEAD_DIM, :] * (1.0 / acc_ref[c, V_HEAD_DIM:V_HEAD_DIM + 1, :])
        o_ref[c * TK:(c + 1) * TK, :] = out.T.astype(BF16)


def _mla_attn(q_t, kn3, kr3, vt4, batch, seq):
    nk = seq // TK
    hv = MLA_HEADS * V_HEAD_DIM
    return pl.pallas_call(
        _attn_kernel,
        grid=(batch, MLA_HEADS, seq // TQ),
        in_specs=[
            pl.BlockSpec((None, QK_HEAD_DIM, TQ), lambda b, h, i: (b, h, i)),
            pl.BlockSpec((None, seq, LANES), lambda b, h, i: (b, 0, h)),
            pl.BlockSpec((None, seq, LANES), lambda b, h, i: (b, 0, 0)),
            pl.BlockSpec((None, seq // TM_KV, V_HEAD_DIM, TM_KV), lambda b, h, i: (b, 0, h, 0)),
        ],
        out_specs=pl.BlockSpec((None, TQ, V_HEAD_DIM), lambda b, h, i: (b, i, h)),
        out_shape=jax.ShapeDtypeStruct((batch, seq, hv), BF16),
        scratch_shapes=[
            pltpu.VMEM((seq, Q_PAD_DIM), BF16),
            pltpu.VMEM((nk, V_AUG_DIM, TK), BF16),
            pltpu.VMEM((DIAG_BLOCKS, Q_PAD_DIM, TK), BF16),
            pltpu.VMEM((TK, TQ), BF16),
            pltpu.VMEM((TK, TQ), BF16),
            pltpu.VMEM((1, TQ), F32),
            pltpu.VMEM((1, TQ), F32),
            pltpu.VMEM((DIAG_BLOCKS, V_AUG_DIM, TK), F32),
            pltpu.VMEM((1, TQ), F32),
            pltpu.VMEM((1, TQ), F32),
            pltpu.VMEM((1, TQ), F32),
        ],
        compiler_params=_params("arbitrary", "arbitrary", "arbitrary"),
        name="mla_attn",
    )(q_t, kn3, kr3, vt4)


def _lru_kernel(x_ref, y_ref, wc_ref, bc_ref, wrg_ref, brg_ref, wig_ref, big_ref, lam_ref,
                o_ref, xprev, a_s, b_s, hc_ref):
    si = pl.program_id(2)
    ts, cb = TS_LRU, CB_LRU

    @pl.when(si == 0)
    def _():
        xprev[...] = jnp.zeros((8, cb), F32)
        hc_ref[...] = jnp.zeros((8, cb), F32)

    x = x_ref[...].astype(F32)
    prev = xprev[...]
    row8 = lax.broadcasted_iota(jnp.int32, (8, cb), 0)
    last = LRU_CONV_WIDTH - 1
    xc = bc_ref[...] + x * wc_ref[last:last + 1, :]
    for k in range(1, LRU_CONV_WIDTH):
        xs = pltpu.roll(x, k, 0)
        head = jnp.where(row8 < k, pltpu.roll(prev, k, 0), xs[0:8, :])
        xs = jnp.concatenate([head, xs[8:, :]], axis=0)
        xc = xc + xs * wc_ref[last - k:last - k + 1, :]
    xprev[...] = x[ts - 8:, :]

    xcb = xc.astype(BF16)
    r = jax.nn.sigmoid(jnp.dot(xcb, wrg_ref[...], preferred_element_type=F32) + brg_ref[...])
    ig = jax.nn.sigmoid(jnp.dot(xcb, wig_ref[...], preferred_element_type=F32) + big_ref[...])
    z = -lam_ref[...]
    softplus = jnp.maximum(z, 0.0) + jnp.log1p(jnp.exp(-jnp.abs(z)))
    a = jnp.exp((-LRU_C * softplus) * r)
    v = 1.0 - a * a
    u = jnp.where(v > 0.0, v * lax.rsqrt(v), 0.0) * (ig * xc)

    ng = ts // 8
    a = a.reshape(ng, 8, cb)
    u = u.reshape(ng, 8, cb)
    row = lax.broadcasted_iota(jnp.int32, (ng, 8, cb), 1)
    for d in (1, 2, 4):
        keep = row >= d
        a_sh = pltpu.roll(a, d, 1)
        u_sh = pltpu.roll(u, d, 1)
        u = jnp.where(keep, a, 0.0) * u_sh + u
        a = a * jnp.where(keep, a_sh, 1.0)
    a_s[...] = a.reshape(ts, cb)
    b_s[...] = u.reshape(ts, cb)

    hc = hc_ref[...]
    for g in range(ts // 8):
        hg = a_s[8 * g:8 * g + 8, :] * hc + b_s[8 * g:8 * g + 8, :]
        b_s[8 * g:8 * g + 8, :] = hg
        hc = jnp.broadcast_to(hg[7:8, :], (8, cb))
    hc_ref[...] = hc

    o_ref[...] = (b_s[...] * _gelu_tanh(y_ref[...].astype(F32))).astype(BF16)


def _lru(proj3, w_conv, b_conv, w_rg2, b_rg, w_ig2, b_ig, lam, batch, seq):
    ts, cb = TS_LRU, CB_LRU
    ncb = LRU_WIDTH // cb
    x_off = 0
    y_off = x_off + ncb
    return pl.pallas_call(
        _lru_kernel,
        grid=(batch, ncb, seq // ts),
        in_specs=[
            pl.BlockSpec((None, ts, cb), lambda b, c, s: (b, s, x_off + c)),
            pl.BlockSpec((None, ts, cb), lambda b, c, s: (b, s, y_off + c)),
            pl.BlockSpec((LRU_CONV_WIDTH, cb), lambda b, c, s: (0, c)),
            pl.BlockSpec((1, cb), lambda b, c, s: (0, c)),
            pl.BlockSpec((None, cb, cb), lambda b, c, s: (c, 0, 0)),
            pl.BlockSpec((1, cb), lambda b, c, s: (0, c)),
            pl.BlockSpec((None, cb, cb), lambda b, c, s: (c, 0, 0)),
            pl.BlockSpec((1, cb), lambda b, c, s: (0, c)),
            pl.BlockSpec((1, cb), lambda b, c, s: (0, c)),
        ],
        out_specs=pl.BlockSpec((None, ts, cb), lambda b, c, s: (b, s, c)),
        out_shape=jax.ShapeDtypeStruct((batch, seq, LRU_WIDTH), BF16),
        scratch_shapes=[
            pltpu.VMEM((8, cb), F32),
            pltpu.VMEM((ts, cb), F32),
            pltpu.VMEM((ts, cb), F32),
            pltpu.VMEM((8, cb), F32),
        ],
        compiler_params=_params("arbitrary", "arbitrary", "arbitrary"),
        name="lru",
    )(proj3, proj3, w_conv, b_conv, w_rg2, b_rg, w_ig2, b_ig, lam)


def _merge_kernel(attn_ref, lru_ref, gm_ref, gl_ref, x_ref, wom_ref, wol_ref, wout_ref, g_ref,
                  o_ref):
    o_mla = jnp.dot(attn_ref[...], wom_ref[...], preferred_element_type=F32)
    o_lru = jnp.dot(lru_ref[...], wol_ref[...], preferred_element_type=F32)
    merged = (jax.nn.sigmoid(gm_ref[...].astype(F32)) * o_mla
              + jax.nn.sigmoid(gl_ref[...].astype(F32)) * o_lru)
    y = jnp.dot(merged.astype(BF16), wout_ref[...], preferred_element_type=F32)
    o_ref[...] = x_ref[...] + _rms(y, g_ref[...])


def _merge(attn2, lru2, proj2, x2d, w_om, w_ol, w_out, g_post):
    t = x2d.shape[0]
    tm = TM_MERGE
    gm_off = 2 * LRU_WIDTH // D_MODEL
    row = lambda i: (i, 0)
    const = lambda i: (0, 0)
    return pl.pallas_call(
        _merge_kernel,
        grid=(t // tm,),
        in_specs=[
            pl.BlockSpec((tm, D_MODEL), row),
            pl.BlockSpec((tm, D_MODEL), row),
            pl.BlockSpec((tm, D_MODEL), lambda i: (i, gm_off)),
            pl.BlockSpec((tm, D_MODEL), lambda i: (i, gm_off + 1)),
            pl.BlockSpec((tm, D_MODEL), row),
            _resident((D_MODEL, D_MODEL), const),
            _resident((D_MODEL, D_MODEL), const),
            _resident((D_MODEL, D_MODEL), const),
            pl.BlockSpec((1, D_MODEL), const),
        ],
        out_specs=pl.BlockSpec((tm, D_MODEL), row),
        out_shape=jax.ShapeDtypeStruct((t, D_MODEL), F32),
        compiler_params=_params("arbitrary"),
        name="merge",
    )(attn2, lru2, proj2, proj2, x2d, w_om, w_ol, w_out, g_post)


def _memkv_kernel(mem_ref, g_ref, wckt_ref, wcv_ref, ckt_ref, cv_ref):
    mn = _rms(mem_ref[...], g_ref[...]).astype(BF16)
    ckt_ref[...] = lax.dot_general(wckt_ref[...], mn, (((1,), (1,)), ((), ())),
                                   preferred_element_type=F32).astype(BF16)
    cv_ref[...] = jnp.dot(mn, wcv_ref[...], preferred_element_type=F32).astype(BF16)


def _mem_kv(mem, g_mem, w_ckt, w_cv):
    b, m, _ = mem.shape
    xd = X_HEADS * X_HEAD_DIM
    return pl.pallas_call(
        _memkv_kernel,
        grid=(b,),
        in_specs=[
            pl.BlockSpec((None, m, D_MODEL), lambda i: (i, 0, 0)),
            pl.BlockSpec((1, D_MODEL), lambda i: (0, 0)),
            pl.BlockSpec((xd, D_MODEL), lambda i: (0, 0)),
            pl.BlockSpec((D_MODEL, xd), lambda i: (0, 0)),
        ],
        out_specs=[
            pl.BlockSpec((None, xd, m), lambda i: (i, 0, 0)),
            pl.BlockSpec((None, m, xd), lambda i: (i, 0, 0)),
        ],
        out_shape=[
            jax.ShapeDtypeStruct((b, xd, m), BF16),
            jax.ShapeDtypeStruct((b, m, xd), BF16),
        ],
        compiler_params=_params("arbitrary"),
        name="mem_kv",
    )(mem, g_mem, w_ckt, w_cv)


def _cross_kernel(x_ref, gpre_ref, wcq_ref, ckt_ref, cv_ref, wco_ref, gpost_ref, o_ref):
    x = x_ref[...]
    h = _rms(x, gpre_ref[...]).astype(BF16)
    cq = jnp.dot(h, wcq_ref[...], preferred_element_type=F32).astype(BF16)
    outs = []
    for hh in range(X_HEADS):
        sl = slice(hh * X_HEAD_DIM, (hh + 1) * X_HEAD_DIM)
        s = jnp.dot(cq[:, sl], ckt_ref[sl, :], preferred_element_type=F32)
        p = jnp.exp2(s - jnp.max(s, axis=-1, keepdims=True))
        l = jnp.sum(p, axis=-1, keepdims=True)
        pv = jnp.dot(p.astype(BF16), cv_ref[:, sl], preferred_element_type=F32)
        outs.append((pv * (1.0 / l)).astype(BF16))
    co = jnp.concatenate(outs, axis=-1)
    y = jnp.dot(co, wco_ref[...], preferred_element_type=F32)
    o_ref[...] = x + _rms(y, gpost_ref[...])


def _cross(x2d, g_pre, w_cq, ckt, cv, w_co, g_post, seq):
    t = x2d.shape[0]
    tm = TM_CROSS
    xd = X_HEADS * X_HEAD_DIM
    m = cv.shape[1]
    per_b = seq // tm
    return pl.pallas_call(
        _cross_kernel,
        grid=(t // tm,),
        in_specs=[
            pl.BlockSpec((tm, D_MODEL), lambda i: (i, 0)),
            pl.BlockSpec((1, D_MODEL), lambda i: (0, 0)),
            pl.BlockSpec((D_MODEL, xd), lambda i: (0, 0)),
            pl.BlockSpec((None, xd, m), lambda i: (i // per_b, 0, 0)),
            pl.BlockSpec((None, m, xd), lambda i: (i // per_b, 0, 0)),
            pl.BlockSpec((xd, D_MODEL), lambda i: (0, 0)),
            pl.BlockSpec((1, D_MODEL), lambda i: (0, 0)),
        ],
        out_specs=pl.BlockSpec((tm, D_MODEL), lambda i: (i, 0)),
        out_shape=jax.ShapeDtypeStruct((t, D_MODEL), F32),
        compiler_params=_params("arbitrary"),
        name="cross",
    )(x2d, g_pre, w_cq, ckt, cv, w_co, g_post)


def _up_kernel(x_ref, halo_ref, g_ref, wg_ref, wv_ref, cwg_ref, cwv_ref, cbg_ref, cbv_ref,
               o_ref, h_ref, *, tiles_per_seq):
    i = pl.program_id(0)
    j = pl.program_id(1)
    tm, halo = TM_UP, HALO_UP

    @pl.when(j == 0)
    def _():
        g = g_ref[...]
        keep = jnp.where(i % tiles_per_seq == 0, 0.0, 1.0)
        h_ref[0:halo, :] = (_rms(halo_ref[...], g) * keep).astype(BF16)
        h_ref[halo:halo + tm, :] = _rms(x_ref[...], g).astype(BF16)

    def conv(w_ref, cw_ref, cb_ref):
        a = jnp.dot(h_ref[...], w_ref[...], preferred_element_type=F32)
        y = a * cw_ref[0:1, :]
        for k in range(1, FFN_CONV_WIDTH):
            y = pltpu.roll(y, 1, 0) + a * cw_ref[k:k + 1, :]
        return y[halo:, :] + cb_ref[...]

    gate = conv(wg_ref, cwg_ref, cbg_ref)
    val = conv(wv_ref, cwv_ref, cbv_ref)
    o_ref[...] = (_gelu_tanh(gate) * val).astype(BF16)


def _ffn_up(x2d, g_pre, w_up, w_fconv, b_fconv, seq):
    t = x2d.shape[0]
    tm, tn, halo = TM_UP, TN_UP, HALO_UP
    nj = D_FF // tn
    return pl.pallas_call(
        functools.partial(_up_kernel, tiles_per_seq=seq // tm),
        grid=(t // tm, nj),
        in_specs=[
            pl.BlockSpec((tm, D_MODEL), lambda i, j: (i, 0)),
            pl.BlockSpec((halo, D_MODEL), lambda i, j: (jnp.maximum(i * (tm // halo) - 1, 0), 0)),
            pl.BlockSpec((1, D_MODEL), lambda i, j: (0, 0)),
            pl.BlockSpec((D_MODEL, tn), lambda i, j: (0, j)),
            pl.BlockSpec((D_MODEL, tn), lambda i, j: (0, j + nj)),
            pl.BlockSpec((FFN_CONV_WIDTH, tn), lambda i, j: (0, j)),
            pl.BlockSpec((FFN_CONV_WIDTH, tn), lambda i, j: (0, j + nj)),
            pl.BlockSpec((1, tn), lambda i, j: (0, j)),
            pl.BlockSpec((1, tn), lambda i, j: (0, j + nj)),
        ],
        out_specs=pl.BlockSpec((tm, tn), lambda i, j: (i, j)),
        out_shape=jax.ShapeDtypeStruct((t, D_FF), BF16),
        scratch_shapes=[pltpu.VMEM((tm + halo, D_MODEL), BF16)],
        compiler_params=_params("arbitrary", "arbitrary"),
        name="ffn_up",
    )(x2d, x2d, g_pre, w_up, w_up, w_fconv, w_fconv, b_fconv, b_fconv)


def _down_kernel(a_ref, x_ref, w_ref, g_ref, o_ref):
    y = jnp.dot(a_ref[...], w_ref[...], preferred_element_type=F32)
    o_ref[...] = x_ref[...] + _rms(y, g_ref[...])


def _ffn_down(act, x2d, w_down, g_post):
    t = x2d.shape[0]
    tm = TM_DOWN
    return pl.pallas_call(
        _down_kernel,
        grid=(t // tm,),
        in_specs=[
            pl.BlockSpec((tm, D_FF), lambda i: (i, 0)),
            pl.BlockSpec((tm, D_MODEL), lambda i: (i, 0)),
            _resident((D_FF, D_MODEL), lambda i: (0, 0)),
            pl.BlockSpec((1, D_MODEL), lambda i: (0, 0)),
        ],
        out_specs=pl.BlockSpec((tm, D_MODEL), lambda i: (i, 0)),
        out_shape=jax.ShapeDtypeStruct((t, D_MODEL), F32),
        compiler_params=_params("arbitrary"),
        name="ffn_down",
    )(act, x2d, w_down, g_post)


def _rope_angles(positions):
    assert QK_ROPE_DIM * 2 == LANES
    inv_freq = ROPE_THETA ** (-jnp.arange(0, QK_ROPE_DIM, 2, dtype=F32) / QK_ROPE_DIM)
    ang = positions.astype(F32).reshape(-1, 1) * inv_freq
    return jnp.cos(ang), jnp.sin(ang)


def _block_diag_pairs(w):
    nb, d, _ = w.shape
    w = w.reshape(nb // 2, 2, d, d)
    z = jnp.zeros_like(w[:, 0])
    top = jnp.concatenate([w[:, 0], z], axis=-1)
    bot = jnp.concatenate([z, w[:, 1]], axis=-1)
    return jnp.concatenate([top, bot], axis=-2)


def kernel(x, mem, positions, g_pre_mix, g_post_mix, w_in, g_ckv, w_ukv, w_o_mla, w_conv_lru, b_conv_lru, w_rg, b_rg, w_ig, b_ig, lru_lambda, w_o_lru, w_out, g_pre_x, g_post_x, g_mem, w_cq, w_ck, w_cv, w_co, g_pre_ffn, g_post_ffn, w_up, w_fconv, b_fconv, w_down):
    batch, seq, d = x.shape
    assert d == D_MODEL and g_pre_mix.shape[0] == 1
    t = batch * seq
    x2d = x.reshape(t, d)
    l = 0

    q_scale = (QK_HEAD_DIM ** -0.5) * LOG2E
    wi_t = w_in[l].T
    wq_t = (wi_t[:Q_COLS] * q_scale).astype(BF16)
    kv_end = Q_COLS + KV_LORA_RANK + QK_ROPE_DIM
    wr_t = wi_t[kv_end:].astype(BF16)
    wb_t = jnp.pad(wi_t[Q_COLS:kv_end], ((0, LANES - QK_ROPE_DIM), (0, 0))).astype(BF16)
    wukv = w_ukv[l].reshape(KV_LORA_RANK, MLA_HEADS, QK_NOPE_DIM + V_HEAD_DIM)
    w_uk = wukv[:, :, :QK_NOPE_DIM].reshape(KV_LORA_RANK, -1).astype(BF16)
    w_uvt = wukv[:, :, QK_NOPE_DIM:].reshape(KV_LORA_RANK, -1).T.astype(BF16)
    row = lambda v: v.reshape(1, -1)
    cos, sin = _rope_angles(positions)

    q_t, proj = _proj_in(x2d, row(g_pre_mix[l]), wq_t, wr_t, cos.T, sin.T, batch, seq)
    kn, kr, vt = _kv_path(x2d, row(g_pre_mix[l]), wb_t, row(g_ckv[l]), w_uk, w_uvt, cos, sin,
                          batch, seq)
    proj3 = proj.reshape(batch, seq, -1)
    attn = _mla_attn(q_t, kn.reshape(batch, seq, -1), kr.reshape(batch, seq, LANES),
                     vt.reshape(batch, seq // TM_KV, MLA_HEADS * V_HEAD_DIM, TM_KV), batch, seq)
    lru = _lru(proj3, w_conv_lru[l], row(b_conv_lru[l]),
               _block_diag_pairs(w_rg[l]).astype(BF16), row(b_rg[l]),
               _block_diag_pairs(w_ig[l]).astype(BF16), row(b_ig[l]),
               row(lru_lambda[l]), batch, seq)
    x1 = _merge(attn.reshape(t, -1), lru.reshape(t, -1), proj, x2d,
                w_o_mla[l].astype(BF16), w_o_lru[l].astype(BF16), w_out[l].astype(BF16),
                row(g_post_mix[l]))

    c_scale = (X_HEAD_DIM ** -0.5) * LOG2E
    ckt, cv = _mem_kv(mem, row(g_mem[l]), w_ck[l].T.astype(BF16), w_cv[l].astype(BF16))
    x2 = _cross(x1, row(g_pre_x[l]), (w_cq[l] * c_scale).astype(BF16), ckt, cv,
                w_co[l].astype(BF16), row(g_post_x[l]), seq)

    act = _ffn_up(x2, row(g_pre_ffn[l]), w_up[l].astype(BF16), w_fconv[l], row(b_fconv[l]), seq)
    x3 = _ffn_down(act, x2, w_down[l].astype(BF16), row(g_post_ffn[l]))
    return x3.reshape(batch, seq, d)
```

```python
import functools
import math

import jax
import jax.numpy as jnp
from jax import lax
from jax.experimental import pallas as pl
from jax.experimental.pallas import tpu as pltpu

F32 = jnp.float32
BF16 = jnp.bfloat16

D_MODEL = 2048
MLA_HEADS = 16
QK_NOPE_DIM = 128
QK_ROPE_DIM = 64
V_HEAD_DIM = 128
KV_LORA_RANK = 512
QK_HEAD_DIM = QK_NOPE_DIM + QK_ROPE_DIM
ROPE_THETA = 10000.0
LRU_WIDTH = D_MODEL
LRU_BLOCKS = 16
LRU_BLOCK_DIM = LRU_WIDTH // LRU_BLOCKS
LRU_CONV_WIDTH = 4
LRU_C = 8.0
X_HEADS = 4
X_HEAD_DIM = 128
D_FF = 5632
FFN_CONV_WIDTH = 3
EPS = 1e-6

LOG2E = math.log2(math.e)
SCORE_SCALE = (QK_HEAD_DIM ** -0.5) * LOG2E
LANES = 128
BF16_ROWS = 16
Q_PAD_DIM = 2 * LANES
V_AUG_DIM = V_HEAD_DIM + BF16_ROWS
Q_COLS = MLA_HEADS * QK_HEAD_DIM
VMEM_LIMIT = 56 * 1024 * 1024
NT_DIMS = (((1,), (1,)), ((), ()))

TM_PROJ, TN_PROJ = 1024, 1024
TN_Q = 4 * QK_HEAD_DIM
TM_KV = 512
TQ, TK = 4096, 1024
DIAG_BLOCKS = TQ // TK
assert TQ % TK == 0 and DIAG_BLOCKS % 2 == 0 and TK % TM_KV == 0
Q_CHUNKS = 8
PROBE_KEYS = BF16_ROWS
SCORE_JUMP_LIMIT = 64.0
TS_LRU, CB_LRU = 512, 256
TM_MERGE = 256
TM_CROSS = 512
TM_UP, TN_UP, HALO_UP = 1024, 512, 16
TM_DOWN = 256


def _params(*sem):
    return pltpu.CompilerParams(dimension_semantics=sem, vmem_limit_bytes=VMEM_LIMIT)


def _resident(shape, index_map):
    return pl.BlockSpec(shape, index_map, pipeline_mode=pl.Buffered(1))


def _rms(x, g):
    return x * lax.rsqrt(jnp.mean(x * x, axis=-1, keepdims=True) + EPS) * g


def _gelu_tanh(x):
    c = math.sqrt(2.0 / math.pi)
    return 0.5 * x * (1.0 + jnp.tanh(c * (x + 0.044715 * (x * x * x))))


def _rope_tables(cos_ref, sin_ref):
    c = cos_ref[...]
    s = sin_ref[...]
    z = jnp.zeros_like(c)
    return jnp.concatenate([c, c, z, z, -s, z, z, z, z, s, z, z], axis=-1)


def _rope_slab(slab, trig):
    c = trig[:, 0:LANES]
    s1 = trig[:, LANES:2 * LANES]
    s2 = trig[:, 2 * LANES:3 * LANES]
    return slab * c + pltpu.roll(slab, 96, 1) * s1 + pltpu.roll(slab, 32, 1) * s2


def _proj_kernel(x_ref, g_ref, wq_ref, wr_ref, cos_ref, sin_ref, qt_ref, o_ref, h_ref, *,
                 n_q_tiles):
    j = pl.program_id(1)

    @pl.when(j == 0)
    def _():
        h_ref[...] = _rms(x_ref[...], g_ref[...]).astype(BF16)

    @pl.when(j >= n_q_tiles)
    def _():
        o_ref[...] = lax.dot_general(h_ref[...], wr_ref[...], NT_DIMS,
                                     preferred_element_type=F32).astype(BF16)

    @pl.when(j < n_q_tiles)
    def _():
        acc = lax.dot_general(wq_ref[...], h_ref[...], NT_DIMS, preferred_element_type=F32)
        cos_t = cos_ref[...]
        sin_t = sin_ref[...]
        half = QK_ROPE_DIM // 2
        for hh in range(TN_Q // QK_HEAD_DIM):
            r0 = hh * QK_HEAD_DIM
            r1 = r0 + QK_NOPE_DIM
            qt_ref[r0:r1, :] = acc[r0:r1, :].astype(BF16)
            x1 = acc[r1:r1 + half, :]
            x2 = acc[r1 + half:r1 + 2 * half, :]
            qt_ref[r1:r1 + half, :] = (x1 * cos_t - x2 * sin_t).astype(BF16)
            qt_ref[r1 + half:r1 + 2 * half, :] = (x2 * cos_t + x1 * sin_t).astype(BF16)


def _proj_in(x2d, g, wq_t, wr_t, cos_t, sin_t, batch, seq):
    t = x2d.shape[0]
    nq = Q_COLS // TN_Q
    nr = wr_t.shape[0] // TN_PROJ
    per_b = seq // TM_PROJ
    half = QK_ROPE_DIM // 2
    return pl.pallas_call(
        functools.partial(_proj_kernel, n_q_tiles=nq),
        grid=(t // TM_PROJ, nq + nr),
        in_specs=[
            pl.BlockSpec((TM_PROJ, D_MODEL), lambda i, j: (i, 0)),
            pl.BlockSpec((1, D_MODEL), lambda i, j: (0, 0)),
            pl.BlockSpec((TN_Q, D_MODEL), lambda i, j: (jnp.minimum(j, nq - 1), 0)),
            pl.BlockSpec((TN_PROJ, D_MODEL), lambda i, j: (jnp.maximum(j - nq, 0), 0)),
            pl.BlockSpec((half, TM_PROJ), lambda i, j: (0, i)),
            pl.BlockSpec((half, TM_PROJ), lambda i, j: (0, i)),
        ],
        out_specs=[
            pl.BlockSpec((None, TN_Q, TM_PROJ),
                         lambda i, j: (i // per_b, jnp.minimum(j, nq - 1), i % per_b)),
            pl.BlockSpec((TM_PROJ, TN_PROJ), lambda i, j: (i, jnp.maximum(j - nq, 0))),
        ],
        out_shape=[
            jax.ShapeDtypeStruct((batch, Q_COLS, seq), BF16),
            jax.ShapeDtypeStruct((t, wr_t.shape[0]), BF16),
        ],
        scratch_shapes=[pltpu.VMEM((TM_PROJ, D_MODEL), BF16)],
        compiler_params=_params("arbitrary", "arbitrary"),
        name="proj_in",
    )(x2d, g, wq_t, wr_t, cos_t, sin_t)


def _kv_kernel(x_ref, g_ref, wb_ref, gc_ref, wuk_ref, wuvt_ref, cos_ref, sin_ref,
               kn_ref, kr_ref, vt_ref):
    h = _rms(x_ref[...], g_ref[...]).astype(BF16)
    c = lax.dot_general(h, wb_ref[...], NT_DIMS, preferred_element_type=F32)
    n = _rms(c[:, :KV_LORA_RANK], gc_ref[...]).astype(BF16)
    kn_ref[...] = jnp.dot(n, wuk_ref[...], preferred_element_type=F32).astype(BF16)
    vt_ref[...] = lax.dot_general(wuvt_ref[...], n, NT_DIMS,
                                  preferred_element_type=F32).astype(BF16)
    kr = _rope_slab(c[:, KV_LORA_RANK:], _rope_tables(cos_ref, sin_ref))
    kr_ref[...] = (kr * SCORE_SCALE).astype(BF16)


def _kv_path(x2d, g, w_b, g_ckv, w_uk, w_uvt, cos, sin, batch, seq):
    t = x2d.shape[0]
    nk = seq // TM_KV
    hv = MLA_HEADS * V_HEAD_DIM
    return pl.pallas_call(
        _kv_kernel,
        grid=(t // TM_KV,),
        in_specs=[
            pl.BlockSpec((TM_KV, D_MODEL), lambda i: (i, 0)),
            pl.BlockSpec((1, D_MODEL), lambda i: (0, 0)),
            _resident(w_b.shape, lambda i: (0, 0)),
            pl.BlockSpec((1, KV_LORA_RANK), lambda i: (0, 0)),
            _resident(w_uk.shape, lambda i: (0, 0)),
            _resident(w_uvt.shape, lambda i: (0, 0)),
            pl.BlockSpec((TM_KV, QK_ROPE_DIM // 2), lambda i: (i, 0)),
            pl.BlockSpec((TM_KV, QK_ROPE_DIM // 2), lambda i: (i, 0)),
        ],
        out_specs=[
            pl.BlockSpec((TM_KV, hv), lambda i: (i, 0)),
            pl.BlockSpec((TM_KV, LANES), lambda i: (i, 0)),
            pl.BlockSpec((None, hv, TM_KV), lambda i: (i, 0, 0)),
        ],
        out_shape=[
            jax.ShapeDtypeStruct((t, hv), BF16),
            jax.ShapeDtypeStruct((t, LANES), BF16),
            jax.ShapeDtypeStruct((batch * nk, hv, TM_KV), BF16),
        ],
        compiler_params=_params("arbitrary"),
        name="kv_path",
    )(x2d, g, w_b, g_ckv, w_uk, w_uvt, cos, sin)


def _attn_kernel(q_ref, kn_ref, kr_ref, vt_ref, o_ref,
                 kcat_ref, vaug_ref, qt_ref, p0, p1, a0, a1, acc_ref, r_ref, rprev_ref, jump_ref):
    i = pl.program_id(2)
    n = DIAG_BLOCKS * (i + 1)

    @pl.when(i == 0)
    def _():
        kcat_ref[:, 0:LANES] = kn_ref[...]
        kcat_ref[:, LANES:Q_PAD_DIM] = kr_ref[...]
        sub = TK // TM_KV
        for j in range(vaug_ref.shape[0]):
            for c in range(sub):
                vaug_ref[j, 0:V_HEAD_DIM, c * TM_KV:(c + 1) * TM_KV] = vt_ref[j * sub + c]
        vaug_ref[:, V_HEAD_DIM:V_AUG_DIM, :] = jnp.ones(
            (vaug_ref.shape[0], V_AUG_DIM - V_HEAD_DIM, TK), BF16)
        qt_ref[:, QK_HEAD_DIM:Q_PAD_DIM, :] = jnp.zeros(
            (DIAG_BLOCKS, Q_PAD_DIM - QK_HEAD_DIM, TK), BF16)

    acc_ref[...] = jnp.zeros(acc_ref.shape, F32)
    jump_ref[...] = jnp.zeros(jump_ref.shape, F32)
    for c in range(DIAG_BLOCKS):
        slab = slice(c * TK, (c + 1) * TK)
        qt_ref[c, 0:QK_HEAD_DIM, :] = q_ref[:, slab]
        s_probe = jnp.dot(kcat_ref[0:PROBE_KEYS, :], qt_ref[c], preferred_element_type=F32)
        key = lax.broadcasted_iota(jnp.int32, s_probe.shape, 0)
        qry = lax.broadcasted_iota(jnp.int32, s_probe.shape, 1) + (i * TQ + c * TK)
        r0 = jnp.max(jnp.where(key <= qry, s_probe, -jnp.inf), axis=0, keepdims=True)
        r_ref[:, slab] = r0
        rprev_ref[:, slab] = r0

    def in_slab(cols):
        c = cols.start // TK
        return c, slice(cols.start - c * TK, cols.stop - c * TK)

    def visible(diag, cols):
        if diag is None:
            return TK, False
        nkeys = max(0, min(TK, cols.stop - diag * TK))
        return nkeys, diag * TK + nkeys - 1 > cols.start

    def qs(b, p_ref, a_ref, diag, cols):
        nkeys, masked = visible(diag, cols)
        if nkeys == 0:
            return
        k0 = pl.multiple_of(b * TK, TK)
        c, sub = in_slab(cols)
        s = jnp.dot(kcat_ref[pl.ds(k0, nkeys), :], qt_ref[c, :, sub],
                    preferred_element_type=F32)
        if masked:
            key = lax.broadcasted_iota(jnp.int32, s.shape, 0) + diag * TK
            qry = lax.broadcasted_iota(jnp.int32, s.shape, 1) + cols.start
            s = jnp.where(key <= qry, s, -jnp.inf)
        r = r_ref[:, cols]
        p_ref[0:nkeys, cols] = jnp.exp2(s - r).astype(BF16)
        a_ref[:, cols] = jnp.exp2(rprev_ref[:, cols] - r)
        rprev_ref[:, cols] = r
        block_max = jnp.max(s, axis=0, keepdims=True)
        r_ref[:, cols] = jnp.maximum(r, block_max)
        jump_ref[:, cols] = jnp.maximum(jump_ref[:, cols], block_max - r)

    def pv(b, p_ref, a_ref, diag, cols):
        nkeys, _ = visible(diag, cols)
        if nkeys == 0:
            return
        c, sub = in_slab(cols)
        acc_ref[c, :, sub] = acc_ref[c, :, sub] * a_ref[:, cols] + jnp.dot(
            vaug_ref[b, :, 0:nkeys], p_ref[0:nkeys, cols],
            preferred_element_type=F32)

    bufs = ((p0, a0), (p1, a1))
    skip = object()

    def step(t, parity, qs_diag=skip, pv_diag=skip):
        p_c, a_c = bufs[parity]
        p_p, a_p = bufs[1 - parity]
        for c in range(Q_CHUNKS):
            cols = slice(c * (TQ // Q_CHUNKS), (c + 1) * (TQ // Q_CHUNKS))
            if qs_diag is not skip:
                qs(t, p_c, a_c, qs_diag, cols)
            if pv_diag is not skip:
                pv(t - 1, p_p, a_p, pv_diag, cols)

    @pl.when(n == DIAG_BLOCKS)
    def _():
        step(0, 0, qs_diag=0)

    @pl.when(n > DIAG_BLOCKS)
    def _():
        step(0, 0, qs_diag=None)

    def steady(t, carry):
        for parity in (0, 1):
            @pl.when(t % 2 == parity)
            def _():
                step(t, parity, qs_diag=None, pv_diag=None)
        return carry

    lax.fori_loop(1, n - DIAG_BLOCKS, steady, 0)

    @pl.when(n > DIAG_BLOCKS)
    def _():
        step(n - DIAG_BLOCKS, 0, qs_diag=0, pv_diag=None)

    for d in range(1, DIAG_BLOCKS):
        step(n - DIAG_BLOCKS + d, d % 2, qs_diag=d, pv_diag=d - 1)
    step(n, 0, pv_diag=DIAG_BLOCKS - 1)

    @pl.when(jnp.max(jump_ref[...]) > SCORE_JUMP_LIMIT)
    def _():
        acc_ref[...] = jnp.zeros(acc_ref.shape, F32)

        def slab(c, carry):
            def block(b, m_old):
                k0 = pl.multiple_of(b * TK, TK)
                s = jnp.dot(kcat_ref[pl.ds(k0, TK), :], qt_ref[c], preferred_element_type=F32)
                key = lax.broadcasted_iota(jnp.int32, s.shape, 0) + b * TK
                qry = lax.broadcasted_iota(jnp.int32, s.shape, 1) + (i * TQ + c * TK)
                s = jnp.where(key <= qry, s, -jnp.inf)
                m_new = jnp.maximum(m_old, jnp.max(s, axis=0, keepdims=True))
                p = jnp.exp2(s - m_new).astype(BF16)
                acc_ref[c] = acc_ref[c] * jnp.exp2(m_old - m_new) + jnp.dot(
                    vaug_ref[b], p, preferred_element_type=F32)
                return m_new

            lax.fori_loop(0, n, block, jnp.full((1, TK), -jnp.inf, F32))
            return carry

        lax.fori_loop(0, DIAG_BLOCKS, slab, 0)

    for c in range(DIAG_BLOCKS):
        out = acc_ref[c, 0:V_HEAD_DIM, :] * (1.0 / acc_ref[c, V_HEAD_DIM:V_HEAD_DIM + 1, :])
        o_ref[c * TK:(c + 1) * TK, :] = out.T.astype(BF16)


def _mla_attn(q_t, kn3, kr3, vt4, batch, seq):
    nk = seq // TK
    hv = MLA_HEADS * V_HEAD_DIM
    return pl.pallas_call(
        _attn_kernel,
        grid=(batch, MLA_HEADS, seq // TQ),
        in_specs=[
            pl.BlockSpec((None, QK_HEAD_DIM, TQ), lambda b, h, i: (b, h, i)),
            pl.BlockSpec((None, seq, LANES), lambda b, h, i: (b, 0, h)),
            pl.BlockSpec((None, seq, LANES), lambda b, h, i: (b, 0, 0)),
            pl.BlockSpec((None, seq // TM_KV, V_HEAD_DIM, TM_KV), lambda b, h, i: (b, 0, h, 0)),
        ],
        out_specs=pl.BlockSpec((None, TQ, V_HEAD_DIM), lambda b, h, i: (b, i, h)),
        out_shape=jax.ShapeDtypeStruct((batch, seq, hv), BF16),
        scratch_shapes=[
            pltpu.VMEM((seq, Q_PAD_DIM), BF16),
            pltpu.VMEM((nk, V_AUG_DIM, TK), BF16),
            pltpu.VMEM((DIAG_BLOCKS, Q_PAD_DIM, TK), BF16),
            pltpu.VMEM((TK, TQ), BF16),
            pltpu.VMEM((TK, TQ), BF16),
            pltpu.VMEM((1, TQ), F32),
            pltpu.VMEM((1, TQ), F32),
            pltpu.VMEM((DIAG_BLOCKS, V_AUG_DIM, TK), F32),
            pltpu.VMEM((1, TQ), F32),
            pltpu.VMEM((1, TQ), F32),
            pltpu.VMEM((1, TQ), F32),
        ],
        compiler_params=_params("arbitrary", "arbitrary", "arbitrary"),
        name="mla_attn",
    )(q_t, kn3, kr3, vt4)


def _lru_kernel(x_ref, y_ref, wc_ref, bc_ref, wrg_ref, brg_ref, wig_ref, big_ref, lam_ref,
                o_ref, xprev, a_s, b_s, hc_ref):
    si = pl.program_id(2)
    ts, cb = TS_LRU, CB_LRU

    @pl.when(si == 0)
    def _():
        xprev[...] = jnp.zeros((8, cb), F32)
        hc_ref[...] = jnp.zeros((8, cb), F32)

    x = x_ref[...].astype(F32)
    prev = xprev[...]
    row8 = lax.broadcasted_iota(jnp.int32, (8, cb), 0)
    last = LRU_CONV_WIDTH - 1
    xc = bc_ref[...] + x * wc_ref[last:last + 1, :]
    for k in range(1, LRU_CONV_WIDTH):
        xs = pltpu.roll(x, k, 0)
        head = jnp.where(row8 < k, pltpu.roll(prev, k, 0), xs[0:8, :])
        xs = jnp.concatenate([head, xs[8:, :]], axis=0)
        xc = xc + xs * wc_ref[last - k:last - k + 1, :]
    xprev[...] = x[ts - 8:, :]

    xcb = xc.astype(BF16)
    r = jax.nn.sigmoid(jnp.dot(xcb, wrg_ref[...], preferred_element_type=F32) + brg_ref[...])
    ig = jax.nn.sigmoid(jnp.dot(xcb, wig_ref[...], preferred_element_type=F32) + big_ref[...])
    z = -lam_ref[...]
    softplus = jnp.maximum(z, 0.0) + jnp.log1p(jnp.exp(-jnp.abs(z)))
    a = jnp.exp((-LRU_C * softplus) * r)
    v = 1.0 - a * a
    u = jnp.where(v > 0.0, v * lax.rsqrt(v), 0.0) * (ig * xc)

    ng = ts // 8
    a = a.reshape(ng, 8, cb)
    u = u.reshape(ng, 8, cb)
    row = lax.broadcasted_iota(jnp.int32, (ng, 8, cb), 1)
    for d in (1, 2, 4):
        keep = row >= d
        a_sh = pltpu.roll(a, d, 1)
        u_sh = pltpu.roll(u, d, 1)
        u = jnp.where(keep, a, 0.0) * u_sh + u
        a = a * jnp.where(keep, a_sh, 1.0)
    a_s[...] = a.reshape(ts, cb)
    b_s[...] = u.reshape(ts, cb)

    hc = hc_ref[...]
    for g in range(ts // 8):
        hg = a_s[8 * g:8 * g + 8, :] * hc + b_s[8 * g:8 * g + 8, :]
        b_s[8 * g:8 * g + 8, :] = hg
        hc = jnp.broadcast_to(hg[7:8, :], (8, cb))
    hc_ref[...] = hc

    o_ref[...] = (b_s[...] * _gelu_tanh(y_ref[...].astype(F32))).astype(BF16)


def _lru(proj3, w_conv, b_conv, w_rg2, b_rg, w_ig2, b_ig, lam, batch, seq):
    ts, cb = TS_LRU, CB_LRU
    ncb = LRU_WIDTH // cb
    x_off = 0
    y_off = x_off + ncb
    return pl.pallas_call(
        _lru_kernel,
        grid=(batch, ncb, seq // ts),
        in_specs=[
            pl.BlockSpec((None, ts, cb), lambda b, c, s: (b, s, x_off + c)),
            pl.BlockSpec((None, ts, cb), lambda b, c, s: (b, s, y_off + c)),
            pl.BlockSpec((LRU_CONV_WIDTH, cb), lambda b, c, s: (0, c)),
            pl.BlockSpec((1, cb), lambda b, c, s: (0, c)),
            pl.BlockSpec((None, cb, cb), lambda b, c, s: (c, 0, 0)),
            pl.BlockSpec((1, cb), lambda b, c, s: (0, c)),
            pl.BlockSpec((None, cb, cb), lambda b, c, s: (c, 0, 0)),
            pl.BlockSpec((1, cb), lambda b, c, s: (0, c)),
            pl.BlockSpec((1, cb), lambda b, c, s: (0, c)),
        ],
        out_specs=pl.BlockSpec((None, ts, cb), lambda b, c, s: (b, s, c)),
        out_shape=jax.ShapeDtypeStruct((batch, seq, LRU_WIDTH), BF16),
        scratch_shapes=[
            pltpu.VMEM((8, cb), F32),
            pltpu.VMEM((ts, cb), F32),
            pltpu.VMEM((ts, cb), F32),
            pltpu.VMEM((8, cb), F32),
        ],
        compiler_params=_params("arbitrary", "arbitrary", "arbitrary"),
        name="lru",
    )(proj3, proj3, w_conv, b_conv, w_rg2, b_rg, w_ig2, b_ig, lam)


def _merge_kernel(attn_ref, lru_ref, gm_ref, gl_ref, x_ref, wom_ref, wol_ref, wout_ref, g_ref,
                  o_ref):
    o_mla = jnp.dot(attn_ref[...], wom_ref[...], preferred_element_type=F32)
    o_lru = jnp.dot(lru_ref[...], wol_ref[...], preferred_element_type=F32)
    merged = (jax.nn.sigmoid(gm_ref[...].astype(F32)) * o_mla
              + jax.nn.sigmoid(gl_ref[...].astype(F32)) * o_lru)
    y = jnp.dot(merged.astype(BF16), wout_ref[...], preferred_element_type=F32)
    o_ref[...] = x_ref[...] + _rms(y, g_ref[...])


def _merge(attn2, lru2, proj2, x2d, w_om, w_ol, w_out, g_post):
    t = x2d.shape[0]
    tm = TM_MERGE
    gm_off = 2 * LRU_WIDTH // D_MODEL
    row = lambda i: (i, 0)
    const = lambda i: (0, 0)
    return pl.pallas_call(
        _merge_kernel,
        grid=(t // tm,),
        in_specs=[
            pl.BlockSpec((tm, D_MODEL), row),
            pl.BlockSpec((tm, D_MODEL), row),
            pl.BlockSpec((tm, D_MODEL), lambda i: (i, gm_off)),
            pl.BlockSpec((tm, D_MODEL), lambda i: (i, gm_off + 1)),
            pl.BlockSpec((tm, D_MODEL), row),
            _resident((D_MODEL, D_MODEL), const),
            _resident((D_MODEL, D_MODEL), const),
            _resident((D_MODEL, D_MODEL), const),
            pl.BlockSpec((1, D_MODEL), const),
        ],
        out_specs=pl.BlockSpec((tm, D_MODEL), row),
        out_shape=jax.ShapeDtypeStruct((t, D_MODEL), F32),
        compiler_params=_params("arbitrary"),
        name="merge",
    )(attn2, lru2, proj2, proj2, x2d, w_om, w_ol, w_out, g_post)


def _memkv_kernel(mem_ref, g_ref, wckt_ref, wcv_ref, ckt_ref, cv_ref):
    mn = _rms(mem_ref[...], g_ref[...]).astype(BF16)
    ckt_ref[...] = lax.dot_general(wckt_ref[...], mn, (((1,), (1,)), ((), ())),
                                   preferred_element_type=F32).astype(BF16)
    cv_ref[...] = jnp.dot(mn, wcv_ref[...], preferred_element_type=F32).astype(BF16)


def _mem_kv(mem, g_mem, w_ckt, w_cv):
    b, m, _ = mem.shape
    xd = X_HEADS * X_HEAD_DIM
    return pl.pallas_call(
        _memkv_kernel,
        grid=(b,),
        in_specs=[
            pl.BlockSpec((None, m, D_MODEL), lambda i: (i, 0, 0)),
            pl.BlockSpec((1, D_MODEL), lambda i: (0, 0)),
            pl.BlockSpec((xd, D_MODEL), lambda i: (0, 0)),
            pl.BlockSpec((D_MODEL, xd), lambda i: (0, 0)),
        ],
        out_specs=[
            pl.BlockSpec((None, xd, m), lambda i: (i, 0, 0)),
            pl.BlockSpec((None, m, xd), lambda i: (i, 0, 0)),
        ],
        out_shape=[
            jax.ShapeDtypeStruct((b, xd, m), BF16),
            jax.ShapeDtypeStruct((b, m, xd), BF16),
        ],
        compiler_params=_params("arbitrary"),
        name="mem_kv",
    )(mem, g_mem, w_ckt, w_cv)


def _cross_kernel(x_ref, gpre_ref, wcq_ref, ckt_ref, cv_ref, wco_ref, gpost_ref, o_ref):
    x = x_ref[...]
    h = _rms(x, gpre_ref[...]).astype(BF16)
    cq = jnp.dot(h, wcq_ref[...], preferred_element_type=F32).astype(BF16)
    outs = []
    for hh in range(X_HEADS):
        sl = slice(hh * X_HEAD_DIM, (hh + 1) * X_HEAD_DIM)
        s = jnp.dot(cq[:, sl], ckt_ref[sl, :], preferred_element_type=F32)
        p = jnp.exp2(s - jnp.max(s, axis=-1, keepdims=True))
        l = jnp.sum(p, axis=-1, keepdims=True)
        pv = jnp.dot(p.astype(BF16), cv_ref[:, sl], preferred_element_type=F32)
        outs.append((pv * (1.0 / l)).astype(BF16))
    co = jnp.concatenate(outs, axis=-1)
    y = jnp.dot(co, wco_ref[...], preferred_element_type=F32)
    o_ref[...] = x + _rms(y, gpost_ref[...])


def _cross(x2d, g_pre, w_cq, ckt, cv, w_co, g_post, seq):
    t = x2d.shape[0]
    tm = TM_CROSS
    xd = X_HEADS * X_HEAD_DIM
    m = cv.shape[1]
    per_b = seq // tm
    return pl.pallas_call(
        _cross_kernel,
        grid=(t // tm,),
        in_specs=[
            pl.BlockSpec((tm, D_MODEL), lambda i: (i, 0)),
            pl.BlockSpec((1, D_MODEL), lambda i: (0, 0)),
            pl.BlockSpec((D_MODEL, xd), lambda i: (0, 0)),
            pl.BlockSpec((None, xd, m), lambda i: (i // per_b, 0, 0)),
            pl.BlockSpec((None, m, xd), lambda i: (i // per_b, 0, 0)),
            pl.BlockSpec((xd, D_MODEL), lambda i: (0, 0)),
            pl.BlockSpec((1, D_MODEL), lambda i: (0, 0)),
        ],
        out_specs=pl.BlockSpec((tm, D_MODEL), lambda i: (i, 0)),
        out_shape=jax.ShapeDtypeStruct((t, D_MODEL), F32),
        compiler_params=_params("arbitrary"),
        name="cross",
    )(x2d, g_pre, w_cq, ckt, cv, w_co, g_post)


def _up_kernel(x_ref, halo_ref, g_ref, wg_ref, wv_ref, cwg_ref, cwv_ref, cbg_ref, cbv_ref,
               o_ref, h_ref, *, tiles_per_seq):
    i = pl.program_id(0)
    j = pl.program_id(1)
    tm, halo = TM_UP, HALO_UP

    @pl.when(j == 0)
    def _():
        g = g_ref[...]
        keep = jnp.where(i % tiles_per_seq == 0, 0.0, 1.0)
        h_ref[0:halo, :] = (_rms(halo_ref[...], g) * keep).astype(BF16)
        h_ref[halo:halo + tm, :] = _rms(x_ref[...], g).astype(BF16)

    def conv(w_ref, cw_ref, cb_ref):
        a = jnp.dot(h_ref[...], w_ref[...], preferred_element_type=F32)
        y = a * cw_ref[0:1, :]
        for k in range(1, FFN_CONV_WIDTH):
            y = pltpu.roll(y, 1, 0) + a * cw_ref[k:k + 1, :]
        return y[halo:, :] + cb_ref[...]

    gate = conv(wg_ref, cwg_ref, cbg_ref)
    val = conv(wv_ref, cwv_ref, cbv_ref)
    o_ref[...] = (_gelu_tanh(gate) * val).astype(BF16)


def _ffn_up(x2d, g_pre, w_up, w_fconv, b_fconv, seq):
    t = x2d.shape[0]
    tm, tn, halo = TM_UP, TN_UP, HALO_UP
    nj = D_FF // tn
    return pl.pallas_call(
        functools.partial(_up_kernel, tiles_per_seq=seq // tm),
        grid=(t // tm, nj),
        in_specs=[
            pl.BlockSpec((tm, D_MODEL), lambda i, j: (i, 0)),
            pl.BlockSpec((halo, D_MODEL), lambda i, j: (jnp.maximum(i * (tm // halo) - 1, 0), 0)),
            pl.BlockSpec((1, D_MODEL), lambda i, j: (0, 0)),
            pl.BlockSpec((D_MODEL, tn), lambda i, j: (0, j)),
            pl.BlockSpec((D_MODEL, tn), lambda i, j: (0, j + nj)),
            pl.BlockSpec((FFN_CONV_WIDTH, tn), lambda i, j: (0, j)),
            pl.BlockSpec((FFN_CONV_WIDTH, tn), lambda i, j: (0, j + nj)),
            pl.BlockSpec((1, tn), lambda i, j: (0, j)),
            pl.BlockSpec((1, tn), lambda i, j: (0, j + nj)),
        ],
        out_specs=pl.BlockSpec((tm, tn), lambda i, j: (i, j)),
        out_shape=jax.ShapeDtypeStruct((t, D_FF), BF16),
        scratch_shapes=[pltpu.VMEM((tm + halo, D_MODEL), BF16)],
        compiler_params=_params("arbitrary", "arbitrary"),
        name="ffn_up",
    )(x2d, x2d, g_pre, w_up, w_up, w_fconv, w_fconv, b_fconv, b_fconv)


def _down_kernel(a_ref, x_ref, w_ref, g_ref, o_ref):
    y = jnp.dot(a_ref[...], w_ref[...], preferred_element_type=F32)
    o_ref[...] = x_ref[...] + _rms(y, g_ref[...])


def _ffn_down(act, x2d, w_down, g_post):
    t = x2d.shape[0]
    tm = TM_DOWN
    return pl.pallas_call(
        _down_kernel,
        grid=(t // tm,),
        in_specs=[
            pl.BlockSpec((tm, D_FF), lambda i: (i, 0)),
            pl.BlockSpec((tm, D_MODEL), lambda i: (i, 0)),
            _resident((D_FF, D_MODEL), lambda i: (0, 0)),
            pl.BlockSpec((1, D_MODEL), lambda i: (0, 0)),
        ],
        out_specs=pl.BlockSpec((tm, D_MODEL), lambda i: (i, 0)),
        out_shape=jax.ShapeDtypeStruct((t, D_MODEL), F32),
        compiler_params=_params("arbitrary"),
        name="ffn_down",
    )(act, x2d, w_down, g_post)


def _angles_kernel(pos_ref, freq_ref, cos_ref, sin_ref):
    ang = pos_ref[...] * freq_ref[...]
    cos_ref[...] = jnp.cos(ang)
    sin_ref[...] = jnp.sin(ang)


def _rope_angles(positions):
    half = QK_ROPE_DIM // 2
    per_row = LANES // half
    inv_freq = ROPE_THETA ** (-jnp.arange(0, QK_ROPE_DIM, 2, dtype=F32) / QK_ROPE_DIM)
    t = positions.size
    rows = t // per_row
    pos = jnp.repeat(positions.astype(F32).reshape(rows, per_row), half, axis=1)
    freq = jnp.tile(inv_freq, per_row).reshape(1, LANES)
    tr = min(rows, 1024)
    cos, sin = pl.pallas_call(
        _angles_kernel,
        grid=(rows // tr,),
        in_specs=[pl.BlockSpec((tr, LANES), lambda i: (i, 0)),
                  pl.BlockSpec((1, LANES), lambda i: (0, 0))],
        out_specs=[pl.BlockSpec((tr, LANES), lambda i: (i, 0))] * 2,
        out_shape=[jax.ShapeDtypeStruct((rows, LANES), F32)] * 2,
        compiler_params=_params("arbitrary"),
        name="rope_angles",
    )(pos, freq)
    return cos.reshape(t, half), sin.reshape(t, half)


def _block_diag_pairs(w):
    nb, d, _ = w.shape
    w = w.reshape(nb // 2, 2, d, d)
    z = jnp.zeros_like(w[:, 0])
    top = jnp.concatenate([w[:, 0], z], axis=-1)
    bot = jnp.concatenate([z, w[:, 1]], axis=-1)
    return jnp.concatenate([top, bot], axis=-2)


def kernel(x, mem, positions, g_pre_mix, g_post_mix, w_in, g_ckv, w_ukv, w_o_mla, w_conv_lru, b_conv_lru, w_rg, b_rg, w_ig, b_ig, lru_lambda, w_o_lru, w_out, g_pre_x, g_post_x, g_mem, w_cq, w_ck, w_cv, w_co, g_pre_ffn, g_post_ffn, w_up, w_fconv, b_fconv, w_down):
    batch, seq, d = x.shape
    assert d == D_MODEL and g_pre_mix.shape[0] == 1
    t = batch * seq
    x2d = x.reshape(t, d)
    l = 0

    wi_t = w_in[l].T.astype(BF16)
    kv_end = Q_COLS + KV_LORA_RANK + QK_ROPE_DIM
    wr_t = wi_t[kv_end:]
    wb_t = jnp.pad(wi_t[Q_COLS:kv_end], ((0, LANES - QK_ROPE_DIM), (0, 0)))
    wukv = w_ukv[l].reshape(KV_LORA_RANK, MLA_HEADS, QK_NOPE_DIM + V_HEAD_DIM)
    w_uk = (wukv[:, :, :QK_NOPE_DIM] * SCORE_SCALE).reshape(KV_LORA_RANK, -1).astype(BF16)
    w_uvt = wukv[:, :, QK_NOPE_DIM:].reshape(KV_LORA_RANK, -1).T.astype(BF16)
    row = lambda v: v.reshape(1, -1)
    cos, sin = _rope_angles(positions)

    q_t, proj = _proj_in(x2d, row(g_pre_mix[l]), wi_t, wr_t, cos.T, sin.T, batch, seq)
    kn, kr, vt = _kv_path(x2d, row(g_pre_mix[l]), wb_t, row(g_ckv[l]), w_uk, w_uvt, cos, sin,
                          batch, seq)
    proj3 = proj.reshape(batch, seq, -1)
    attn = _mla_attn(q_t, kn.reshape(batch, seq, -1), kr.reshape(batch, seq, LANES),
                     vt.reshape(batch, seq // TM_KV, MLA_HEADS * V_HEAD_DIM, TM_KV), batch, seq)
    lru = _lru(proj3, w_conv_lru[l], row(b_conv_lru[l]),
               _block_diag_pairs(w_rg[l]).astype(BF16), row(b_rg[l]),
               _block_diag_pairs(w_ig[l]).astype(BF16), row(b_ig[l]),
               row(lru_lambda[l]), batch, seq)
    x1 = _merge(attn.reshape(t, -1), lru.reshape(t, -1), proj, x2d,
                w_o_mla[l].astype(BF16), w_o_lru[l].astype(BF16), w_out[l].astype(BF16),
                row(g_post_mix[l]))

    c_scale = (X_HEAD_DIM ** -0.5) * LOG2E
    ckt, cv = _mem_kv(mem, row(g_mem[l]), w_ck[l].T.astype(BF16), w_cv[l].astype(BF16))
    x2 = _cross(x1, row(g_pre_x[l]), (w_cq[l] * c_scale).astype(BF16), ckt, cv,
                w_co[l].astype(BF16), row(g_post_x[l]), seq)

    act = _ffn_up(x2, row(g_pre_ffn[l]), w_up[l].astype(BF16), w_fconv[l], row(b_fconv[l]), seq)
    x3 = _ffn_down(act, x2, w_down[l].astype(BF16), row(g_post_ffn[l]))
    return x3.reshape(batch, seq, d)
```
